```python
import functools
import jax
import jax.numpy as jnp
from jax import lax
import numpy as np

D_MODEL = 1024
BATCH = 8
SEQ = 4096
DEPTH = 2
DEC_BATCH = 32
DEC_SEQ = 1
PAST_LEN = 16384
PAGE_SIZE = 128

CHUNK = 128
D_A = D_MODEL // 2
G_A = 8
H_B = 8
HD_B = 64
D_B = H_B * HD_B
H_IDX = 8
D_IDX = 64
TOPK_MAX = 256
QBLOCK = 128
H_C = 8
N_C = 64
D_C = H_C * N_C
LORA_W = 64
LORA_A = 64
LORA_G = 128
C_IN = 3 * D_C + LORA_W + LORA_A + LORA_G
N_BRANCH = 3
D_FF = -(-8 * D_MODEL // (3 * 256)) * 256

IN_WIDTHS = (D_A, D_A, D_B, D_B, D_B, H_IDX * D_IDX, D_IDX, H_IDX, C_IN, N_BRANCH * D_MODEL)
N_IN = 2 * D_A + 3 * D_B + H_IDX * D_IDX + D_IDX + H_IDX + C_IN + N_BRANCH * D_MODEL
C_WIDTHS = (D_C, D_C, D_C, LORA_W, LORA_A, LORA_G)
RMS_EPS = 1e-6
LN_EPS = 1e-5
LNX_EPS = 64e-5
ROPE_THETA = 10000.0

kernel_name = 'hybrid_sgu_dsa_rwkv7_decoder_step'

F32 = jnp.float32


def split_last(x, widths):
    idx = np.cumsum(np.array(widths))[:-1].tolist()
    return jnp.split(x, idx, axis=-1)


def rms_norm(x, g):
    xf = x.astype(F32)
    y = xf * lax.rsqrt(jnp.mean(xf * xf, axis=-1, keepdims=True) + RMS_EPS)
    return (y * g.astype(F32)).astype(x.dtype)


def layer_norm(x, g, b, eps):
    xf = x.astype(F32)
    mu = jnp.mean(xf, axis=-1, keepdims=True)
    var = jnp.mean(jnp.square(xf - mu), axis=-1, keepdims=True)
    return ((xf - mu) * lax.rsqrt(var + eps) * g.astype(F32) + b.astype(F32)).astype(x.dtype)


def rope(x, pos):
    half = x.shape[-1] // 2
    inv_freq = ROPE_THETA ** (-jnp.arange(half, dtype=F32) / half)
    ang = pos.astype(F32)[:, None] * inv_freq[None, :]
    cos = jnp.cos(ang)[None, :, None, :]
    sin = jnp.sin(ang)[None, :, None, :]
    xf = x.astype(F32)
    x1, x2 = xf[..., :half], xf[..., half:]
    return jnp.concatenate([x1 * cos - x2 * sin, x2 * cos + x1 * sin], axis=-1).astype(x.dtype)


def take_rows(a, idx):
    return jax.vmap(lambda ab, ib: ab[ib])(a, idx)


def chunk_spatial_gate(u, v, ln_g, ln_b, w_s, b_s):
    v = layer_norm(v, ln_g, ln_b, LN_EPS)
    Bn, T, _ = v.shape
    n_c = -(-T // CHUNK)
    vp = jnp.pad(v, ((0, 0), (0, n_c * CHUNK - T), (0, 0))).reshape(Bn, n_c, CHUNK, G_A, D_A // G_A)
    causal = jnp.tril(jnp.ones((CHUNK, CHUNK), dtype=bool))
    ws = jnp.where(causal[None], w_s, jnp.zeros_like(w_s))
    mixed = jnp.einsum('gij,bcjgd->bcigd', ws, vp) + b_s.T[None, None, :, :, None]
    mixed = mixed.reshape(Bn, n_c * CHUNK, D_A)[:, :T]
    return u * mixed, v


def indexer_scores(qi, ki, wi):
    s = jnp.einsum('bthd,bld->bthl', qi.astype(F32), ki.astype(F32)) * (D_IDX ** -0.5)
    return jnp.einsum('bthl,bth->btl', jax.nn.relu(s), wi.astype(F32) * (H_IDX ** -0.5))


def sparse_attend(q, kg, vg, valid):
    logits = jnp.einsum('bthd,btkhd->bthk', q.astype(F32), kg.astype(F32)) * (HD_B ** -0.5)
    logits = jnp.where(valid[:, :, None, :], logits, -jnp.inf)
    p = jax.nn.softmax(logits, axis=-1)
    return jnp.einsum('bthk,btkhd->bthd', p.astype(vg.dtype), vg).astype(q.dtype)


def dsa_prompt_attend(q, k, v, qi, ki, wi):
    Bn, S = q.shape[:2]
    top = min(TOPK_MAX, S // 4)
    s_pos = jnp.arange(S)

    def block(i):
        q0 = i * QBLOCK
        sl = lambda t: lax.dynamic_slice_in_dim(t, q0, QBLOCK, axis=1)
        t_pos = q0 + jnp.arange(QBLOCK)
        score = indexer_scores(sl(qi), ki, sl(wi))
        score = jnp.where(s_pos[None, None, :] <= t_pos[None, :, None], score, -jnp.inf)
        _, idx = lax.top_k(score, top)
        valid = idx <= t_pos[None, :, None]
        return sparse_attend(sl(q), take_rows(k, idx), take_rows(v, idx), valid)

    out = lax.map(block, jnp.arange(S // QBLOCK))
    return jnp.moveaxis(out, 0, 1).reshape(Bn, S, H_B, HD_B)


def dsa_sample_attend(q, k_new, v_new, qi, ki_new, wi, cache_k, cache_v, cache_kidx, page_table, layer):
    DB, T = q.shape[:2]
    n_keys = PAST_LEN + T
    top = min(TOPK_MAX, n_keys // 4)
    ki_past = cache_kidx[layer, page_table].reshape(DB, PAST_LEN, D_IDX)
    ki_all = jnp.concatenate([ki_past.astype(ki_new.dtype), ki_new], axis=1)
    t_pos = PAST_LEN + jnp.arange(T)
    score = indexer_scores(qi, ki_all, wi)
    score = jnp.where(jnp.arange(n_keys)[None, None, :] <= t_pos[None, :, None], score, -jnp.inf)
    _, idx = lax.top_k(score, top)
    valid = idx <= t_pos[None, :, None]
    in_past = idx < PAST_LEN
    pidx = jnp.minimum(idx, PAST_LEN - 1)
    logical_page = pidx // PAGE_SIZE
    phys = jnp.take_along_axis(page_table, logical_page.reshape(DB, -1), axis=1).reshape(pidx.shape)
    off = pidx % PAGE_SIZE
    nidx = jnp.clip(idx - PAST_LEN, 0, T - 1)
    sel = in_past[..., None, None]
    kg = jnp.where(sel, cache_k[layer, phys, off].astype(k_new.dtype), take_rows(k_new, nidx))
    vg = jnp.where(sel, cache_v[layer, phys, off].astype(v_new.dtype), take_rows(v_new, nidx))
    return sparse_attend(q, kg, vg, valid)


def wkv_step(S, inp):
    r, w, k, v, a, b = inp
    S = (S * w[:, :, None, :]
         + jnp.einsum('bhvk,bhk->bhv', S, a)[..., None] * b[:, :, None, :]
         + v[..., None] * k[:, :, None, :])
    return S, jnp.einsum('bhvk,bhk->bhv', S, r)


def rwkv7_time_mix(pc, prev_row, wkv0, P):
    Bn, T, _ = pc.shape
    prev = jnp.concatenate([prev_row[:, None, :].astype(pc.dtype), pc[:, :-1]], axis=1)
    xm = pc + (prev - pc) * P['c_mu']
    r, k, v, lw, la, lg = split_last(xm, C_WIDTHS)
    w_log = -jax.nn.softplus(-(P['c_w0'] + jnp.tanh(lw) @ P['c_w2']).astype(F32)) - 0.5
    decay = jnp.exp(-jnp.exp(w_log))
    a = jax.nn.sigmoid((P['c_a0'] + la @ P['c_a2']).astype(F32))
    g = jax.nn.sigmoid(lg) @ P['c_g2']

    def heads(t):
        return t.astype(F32).reshape(Bn, T, H_C, N_C)

    r, k, v, a, decay = heads(r), heads(k), heads(v), heads(a), heads(decay)
    kk = k * P['c_kk'].astype(F32).reshape(H_C, N_C)
    kk = kk * lax.rsqrt(jnp.maximum(jnp.sum(kk * kk, axis=-1, keepdims=True), 1e-24))
    k = k * (1.0 + (a - 1.0) * P['c_ka'].astype(F32).reshape(H_C, N_C))
    xs = tuple(jnp.moveaxis(t, 1, 0) for t in (r, decay, k, v, -kk, kk * a))
    wkv, y = lax.scan(wkv_step, wkv0.astype(F32), xs)
    y = jnp.moveaxis(y, 0, 1)
    y = layer_norm(y, P['c_lnx_g'].reshape(H_C, N_C), P['c_lnx_b'].reshape(H_C, N_C), LNX_EPS)
    y = y + jnp.sum(r * k * P['c_rk'].astype(F32), axis=-1, keepdims=True) * v
    y = y.reshape(Bn, T, D_C).astype(pc.dtype) * g
    return y, pc[:, -1], wkv


def trunk_layer(x, pos, P, attend, c_prev, wkv0):
    Bn, T, _ = x.shape
    h = rms_norm(x, P['g_pre_mix'])
    p = h @ P['w_in']
    a_u, a_v, b_q, b_k, b_v, b_qi, b_ki, b_wi, p_c, p_gate = split_last(p, IN_WIDTHS)
    y_a, v_a = chunk_spatial_gate(a_u, a_v, P['a_ln_g'], P['a_ln_b'], P['a_ws'], P['a_bs'])
    q = rope(b_q.reshape(Bn, T, H_B, HD_B), pos)
    k = rope(b_k.reshape(Bn, T, H_B, HD_B), pos)
    v = b_v.reshape(Bn, T, H_B, HD_B)
    qi = rope(b_qi.reshape(Bn, T, H_IDX, D_IDX), pos)
    ki = rope(b_ki[:, :, None, :], pos)[:, :, 0]
    y_b = attend(q, k, v, qi, ki, b_wi).reshape(Bn, T, D_B)
    y_c, c_last, wkv = rwkv7_time_mix(p_c, c_prev, wkv0, P)
    g_a, g_b, g_c = jnp.split(jax.nn.sigmoid(p_gate), N_BRANCH, axis=-1)
    merged = g_a * (y_a @ P['w_br_a']) + g_b * (y_b @ P['w_br_b']) + g_c * (y_c @ P['w_br_c'])
    x = x + rms_norm(merged @ P['w_out'], P['g_post_mix'])
    h = rms_norm(x, P['g_pre_ffn'])
    f1, f3 = jnp.split(h @ P['w_ffn_in'], 2, axis=-1)
    x = x + rms_norm((jax.nn.silu(f1) * f3) @ P['w_ffn_out'], P['g_post_ffn'])
    return x, (k, v, ki, v_a, c_last, wkv)


def setup_inputs(seed: int = 0) -> dict:
    key = jax.random.key(seed)
    ks = iter(jax.random.split(key, 48))

    def nrm(shape, scale=1.0):
        return jax.random.normal(next(ks), shape, dtype=F32) * scale

    def gain(shape):
        return 1.0 + nrm(shape, 0.05)

    n_pages = PAST_LEN // PAGE_SIZE
    n_used = DEC_BATCH * n_pages
    n_pool = n_used + n_used // 4
    L = DEPTH
    x_prompt = nrm((BATCH, SEQ, D_MODEL))
    x_sample = nrm((DEC_BATCH, DEC_SEQ, D_MODEL))
    cache_k = nrm((L, n_pool, PAGE_SIZE, H_B, HD_B))
    cache_v = nrm((L, n_pool, PAGE_SIZE, H_B, HD_B))
    cache_kidx = nrm((L, n_pool, PAGE_SIZE, D_IDX))
    state_c_shift = nrm((L, DEC_BATCH, C_IN))
    state_c_wkv = nrm((L, DEC_BATCH, H_C, N_C, N_C), 0.3)
    page_table = jax.random.permutation(next(ks), n_pool)[:n_used].reshape(DEC_BATCH, n_pages).astype(jnp.int32)
    return {
        'x_prompt': x_prompt,
        'x_sample': x_sample,
        'cache_k': cache_k,
        'cache_v': cache_v,
        'cache_kidx': cache_kidx,
        'state_c_shift': state_c_shift,
        'state_c_wkv': state_c_wkv,
        'page_table': page_table,
        'g_pre_mix': gain((L, D_MODEL)),
        'g_post_mix': gain((L, D_MODEL)),
        'g_pre_ffn': gain((L, D_MODEL)),
        'g_post_ffn': gain((L, D_MODEL)),
        'w_in': nrm((L, D_MODEL, N_IN), D_MODEL ** -0.5),
        'a_ln_g': gain((L, D_A)),
        'a_ln_b': nrm((L, D_A), 0.05),
        'a_ws': nrm((L, G_A, CHUNK, CHUNK), CHUNK ** -0.5),
        'a_bs': 1.0 + nrm((L, G_A, CHUNK), 0.1),
        'c_mu': jax.random.uniform(next(ks), (L, C_IN), dtype=F32),
        'c_w0': nrm((L, D_C), 0.5),
        'c_w2': nrm((L, LORA_W, D_C), 0.5 * LORA_W ** -0.5),
        'c_a0': nrm((L, D_C), 0.1),
        'c_a2': nrm((L, LORA_A, D_C), 0.5 * LORA_A ** -0.5),
        'c_g2': nrm((L, LORA_G, D_C), LORA_G ** -0.5),
        'c_kk': 0.85 + nrm((L, D_C), 0.05),
        'c_ka': 1.0 + nrm((L, D_C), 0.05),
        'c_rk': nrm((L, H_C, N_C), 0.1),
        'c_lnx_g': gain((L, D_C)),
        'c_lnx_b': nrm((L, D_C), 0.05),
        'w_br_a': nrm((L, D_A, D_MODEL), D_A ** -0.5),
        'w_br_b': nrm((L, D_B, D_MODEL), D_B ** -0.5),
        'w_br_c': nrm((L, D_C, D_MODEL), D_C ** -0.5),
        'w_out': nrm((L, D_MODEL, D_MODEL), D_MODEL ** -0.5),
        'w_ffn_in': nrm((L, D_MODEL, 2 * D_FF), D_MODEL ** -0.5),
        'w_ffn_out': nrm((L, D_FF, D_MODEL), D_FF ** -0.5),
    }


def reference(x_prompt, x_sample, cache_k, cache_v, cache_kidx, state_c_shift, state_c_wkv, page_table,
              g_pre_mix, g_post_mix, g_pre_ffn, g_post_ffn, w_in, a_ln_g, a_ln_b, a_ws, a_bs,
              c_mu, c_w0, c_w2, c_a0, c_a2, c_g2, c_kk, c_ka, c_rk, c_lnx_g, c_lnx_b,
              w_br_a, w_br_b, w_br_c, w_out, w_ffn_in, w_ffn_out):
    pos_p = jnp.arange(x_prompt.shape[1], dtype=jnp.int32)
    pos_s = PAST_LEN + jnp.arange(x_sample.shape[1], dtype=jnp.int32)
    hp, hs = x_prompt, x_sample
    kp_l, vp_l, kip_l, cp_l, wp_l = [], [], [], [], []
    ks_l, vs_l, kis_l, vas_l, cs_l, ws_l = [], [], [], [], [], []
    for l in range(DEPTH):
        P = dict(g_pre_mix=g_pre_mix[l], g_post_mix=g_post_mix[l], g_pre_ffn=g_pre_ffn[l],
                 g_post_ffn=g_post_ffn[l], w_in=w_in[l], a_ln_g=a_ln_g[l], a_ln_b=a_ln_b[l],
                 a_ws=a_ws[l], a_bs=a_bs[l], c_mu=c_mu[l], c_w0=c_w0[l], c_w2=c_w2[l], c_a0=c_a0[l],
                 c_a2=c_a2[l], c_g2=c_g2[l], c_kk=c_kk[l], c_ka=c_ka[l], c_rk=c_rk[l],
                 c_lnx_g=c_lnx_g[l], c_lnx_b=c_lnx_b[l], w_br_a=w_br_a[l], w_br_b=w_br_b[l],
                 w_br_c=w_br_c[l], w_out=w_out[l], w_ffn_in=w_ffn_in[l], w_ffn_out=w_ffn_out[l])
        c0 = jnp.zeros((hp.shape[0], C_IN), hp.dtype)
        s0 = jnp.zeros((hp.shape[0], H_C, N_C, N_C), F32)
        hp, (kp, vp, kip, _, cp, wp) = trunk_layer(hp, pos_p, P, dsa_prompt_attend, c0, s0)
        kp_l.append(kp); vp_l.append(vp); kip_l.append(kip); cp_l.append(cp); wp_l.append(wp)
        attend_s = functools.partial(dsa_sample_attend, cache_k=cache_k, cache_v=cache_v,
                                     cache_kidx=cache_kidx, page_table=page_table, layer=l)
        hs, (k_s, v_s, ki_s, va_s, c_s, w_s) = trunk_layer(hs, pos_s, P, attend_s, state_c_shift[l], state_c_wkv[l])
        ks_l.append(k_s); vs_l.append(v_s); kis_l.append(ki_s); vas_l.append(va_s); cs_l.append(c_s); ws_l.append(w_s)
    return (hp, hs,
            jnp.stack(kp_l), jnp.stack(vp_l), jnp.stack(kip_l), jnp.stack(cp_l), jnp.stack(wp_l),
            jnp.stack(ks_l), jnp.stack(vs_l), jnp.stack(kis_l), jnp.stack(vas_l), jnp.stack(cs_l), jnp.stack(ws_l))
```

```python
import functools

import jax
import jax.numpy as jnp
from jax import lax
from jax.experimental import pallas as pl
from jax.experimental.pallas import tpu as pltpu

F32 = jnp.float32
BF16 = jnp.bfloat16
I32 = jnp.int32

D_MODEL = 1024
PAGE_SIZE = 128
CHUNK = 128
D_A = 512
G_A = 8
H_B = 8
HD_B = 64
D_B = 512
H_IDX = 8
D_IDX = 64
TOPK_MAX = 256
QBLOCK = 128
H_C = 8
N_C = 64
D_C = 512
LORA_W = 64
LORA_A = 64
LORA_G = 128
C_IN = 3 * D_C + LORA_W + LORA_A + LORA_G
D_FF = 2816
RMS_EPS = 1e-6
LN_EPS = 1e-5
LNX_EPS = 64e-5
ROPE_THETA = 10000.0

OFF_C = 0
OFF_GATE = 2048
OFF_AU = 5120
OFF_AV = 5632
OFF_Q = 6144
OFF_K = 6656
OFF_V = 7168
OFF_QI = 7680
OFF_KIWI = 8192
N_PAD = 8320

LANE = 128
SCAN_CHUNK = 64
INT_MIN = -2147483648
NEG_BIG = -1e30
VMEM_LIMIT = 56 * 1024 * 1024


def _cparams(sem):
    return pltpu.CompilerParams(dimension_semantics=sem, vmem_limit_bytes=VMEM_LIMIT)


def _mm(a, b):
    return jnp.dot(a.astype(BF16), b.astype(BF16), preferred_element_type=F32)


def _mm_nt(a, b):
    return lax.dot_general(a.astype(BF16), b.astype(BF16), (((1,), (1,)), ((), ())),
                           preferred_element_type=F32)


def _mm_tn(a, b):
    return lax.dot_general(a.astype(BF16), b.astype(BF16), (((0,), (0,)), ((), ())),
                           preferred_element_type=F32)


def _split3(x):
    hi = x.astype(BF16)
    r1 = x - hi.astype(F32)
    mid = r1.astype(BF16)
    lo = (r1 - mid.astype(F32)).astype(BF16)
    return hi, mid, lo


def _mm_exact_rhs(a_bf16, x):
    hi, mid, lo = _split3(x)
    dot = functools.partial(jnp.dot, preferred_element_type=F32)
    return dot(a_bf16, hi) + dot(a_bf16, mid) + dot(a_bf16, lo)


def _mm_exact_lhs(x, b_bf16):
    hi, mid, lo = _split3(x)
    dot = functools.partial(jnp.dot, preferred_element_type=F32)
    return dot(hi, b_bf16) + dot(mid, b_bf16) + dot(lo, b_bf16)


def _sigmoid(x):
    return 1.0 / (1.0 + jnp.exp(-x))


def _rms(x, g):
    return x * lax.rsqrt(jnp.mean(x * x, axis=-1, keepdims=True) + RMS_EPS) * g


def _proj_kernel(x_ref, g_ref, w_ref, o_ref, h_scr):
    @pl.when(pl.program_id(1) == 0)
    def _():
        h_scr[...] = _rms(x_ref[...], g_ref[...]).astype(BF16)

    o_ref[...] = jnp.dot(h_scr[...], w_ref[...], preferred_element_type=F32)


def _proj(x, g, w, tm, tn):
    m = x.shape[0]
    n = w.shape[1]
    return pl.pallas_call(
        _proj_kernel,
        grid=(m // tm, n // tn),
        in_specs=[pl.BlockSpec((tm, D_MODEL), lambda i, j: (i, 0)),
                  pl.BlockSpec((1, D_MODEL), lambda i, j: (0, 0)),
                  pl.BlockSpec((D_MODEL, tn), lambda i, j: (0, j))],
        out_specs=pl.BlockSpec((tm, tn), lambda i, j: (i, j)),
        out_shape=jax.ShapeDtypeStruct((m, n), F32),
        scratch_shapes=[pltpu.VMEM((tm, D_MODEL), BF16)],
        compiler_params=_cparams(("parallel", "arbitrary")),
        name="proj",
    )(x, g, w)


def _swap_halves(x):
    n = x.shape[-1]
    left = pltpu.roll(x, n - 32, x.ndim - 1)
    right = pltpu.roll(x, 32, x.ndim - 1)
    lane = lax.broadcasted_iota(I32, x.shape, x.ndim - 1)
    return jnp.where((lane & 32) == 0, left, right)


def _rope_kernel(q_ref, k_ref, v_ref, qi_ref, kiwi_ref, cos_ref, sin_ref,
                 qo_ref, ko_ref, kb_ref, vb_ref, qio_ref, kiwio_ref):
    cos1 = cos_ref[...]
    sin1 = sin_ref[...]
    cos = jnp.concatenate([cos1] * 4, axis=1)
    sin = jnp.concatenate([sin1] * 4, axis=1)

    def rope(x, c, s):
        return x * c + _swap_halves(x) * s

    qo_ref[0] = (rope(q_ref[0], cos, sin) * (HD_B ** -0.5)).astype(BF16)
    k = rope(k_ref[0], cos, sin)
    ko_ref[0] = k
    kb_ref[0] = k.astype(BF16)
    vb_ref[0] = v_ref[0].astype(BF16)
    qio_ref[0] = rope(qi_ref[0], cos, sin).astype(BF16)
    kiwi = kiwi_ref[0]
    lane = lax.broadcasted_iota(I32, kiwi.shape, 1)
    kiwio_ref[0] = jnp.where(lane < D_IDX, rope(kiwi, cos1, sin1),
                             kiwi * ((D_IDX ** -0.5) * (H_IDX ** -0.5)))


def _rope(p, cos, sin, tm):
    b, t, _ = p.shape

    def col(off, w):
        return pl.BlockSpec((1, tm, w), lambda bi, i, o=off // w: (bi, i, o))

    row = lambda w: pl.BlockSpec((1, tm, w), lambda bi, i: (bi, i, 0))
    tab = pl.BlockSpec((tm, LANE), lambda bi, i: (i, 0))
    return pl.pallas_call(
        _rope_kernel,
        grid=(b, t // tm),
        in_specs=[col(OFF_Q, D_B), col(OFF_K, D_B), col(OFF_V, D_B), col(OFF_QI, D_B), col(OFF_KIWI, LANE),
                  tab, tab],
        out_specs=[row(D_B), row(D_B), row(D_B), row(D_B), row(D_B), row(LANE)],
        out_shape=[jax.ShapeDtypeStruct((b, t, D_B), BF16),
                   jax.ShapeDtypeStruct((b, t, D_B), F32),
                   jax.ShapeDtypeStruct((b, t, D_B), BF16),
                   jax.ShapeDtypeStruct((b, t, D_B), BF16),
                   jax.ShapeDtypeStruct((b, t, D_B), BF16),
                   jax.ShapeDtypeStruct((b, t, LANE), F32)],
        compiler_params=_cparams(("parallel", "parallel")),
        name="rope",
    )(p, p, p, p, p, cos, sin)


def _layer_norm(x, g, b, eps):
    mu = jnp.mean(x, axis=-1, keepdims=True)
    xc = x - mu
    var = jnp.mean(xc * xc, axis=-1, keepdims=True)
    return xc * lax.rsqrt(var + eps) * g + b


def _sgu_kernel(u_ref, v_ref, g_ref, b_ref, ws_ref, bs_ref, o_ref, *, n_chunks):
    v = _layer_norm(v_ref[0], g_ref[...], b_ref[...], LN_EPS).astype(BF16)
    bs = bs_ref[...]
    for c in range(n_chunks):
        rows = slice(c * CHUNK, (c + 1) * CHUNK)
        vc = v[rows]
        mixed = jnp.concatenate(
            [jnp.dot(ws_ref[g], vc[:, g * 64:(g + 1) * 64], preferred_element_type=F32) for g in range(G_A)],
            axis=1)
        o_ref[0, rows, :] = u_ref[0, rows, :] * (mixed + bs)


def _sgu(p, ln_g, ln_b, ws_tril, bs_full, n_chunks):
    b, t, _ = p.shape
    tm = n_chunks * CHUNK
    const = lambda shape: pl.BlockSpec(shape, lambda bi, i: (0,) * len(shape))
    return pl.pallas_call(
        functools.partial(_sgu_kernel, n_chunks=n_chunks),
        grid=(b, t // tm),
        in_specs=[pl.BlockSpec((1, tm, D_A), lambda bi, i: (bi, i, OFF_AU // D_A)),
                  pl.BlockSpec((1, tm, D_A), lambda bi, i: (bi, i, OFF_AV // D_A)),
                  const((1, D_A)), const((1, D_A)), const((G_A, CHUNK, CHUNK)), const((CHUNK, D_A))],
        out_specs=pl.BlockSpec((1, tm, D_A), lambda bi, i: (bi, i, 0)),
        out_shape=jax.ShapeDtypeStruct((b, t, D_A), F32),
        compiler_params=_cparams(("parallel", "parallel")),
        name="sgu",
    )(p, p, ln_g, ln_b, ws_tril, bs_full)


def _sgu_step_kernel(u_ref, v_ref, g_ref, b_ref, w0_ref, b0_ref, y_ref, vo_ref):
    v = _layer_norm(v_ref[...], g_ref[...], b_ref[...], LN_EPS)
    vo_ref[...] = v
    y_ref[...] = u_ref[...] * (v * w0_ref[...] + b0_ref[...])


def _sgu_step(p, ln_g, ln_b, w0, b0):
    m = p.shape[0]
    const = pl.BlockSpec((1, D_A), lambda i: (0, 0))
    return pl.pallas_call(
        _sgu_step_kernel,
        grid=(1,),
        in_specs=[pl.BlockSpec((m, D_A), lambda i: (0, OFF_AU // D_A)),
                  pl.BlockSpec((m, D_A), lambda i: (0, OFF_AV // D_A)),
                  const, const, const, const],
        out_specs=[pl.BlockSpec((m, D_A), lambda i: (0, 0))] * 2,
        out_shape=[jax.ShapeDtypeStruct((m, D_A), F32)] * 2,
        name="sgu_step",
    )(p, p, ln_g, ln_b, w0, b0)


def _sortable(x):
    i = pltpu.bitcast(x + 0.0, I32)
    return i ^ ((i >> 31) & 0x7FFFFFFF)


def _kth_threshold(count_ge, k, shape):
    res = jnp.where(count_ge(jnp.zeros(shape, I32)) >= k, 0, INT_MIN).astype(I32)

    def body(it, res):
        cand = res + lax.shift_left(jnp.int32(1), 30 - it)
        return jnp.where(count_ge(cand) >= k, cand, res)

    return lax.fori_loop(0, 31, body, res)


def _dsa_kernel(q_ref, qi_ref, kiwiq_ref, k_ref, v_ref, kiwi_ref, o_ref, key_scr, bias_scr, wb_scr, *, topk):
    i = pl.program_id(1)
    nt = i + 1
    row = lax.broadcasted_iota(I32, (QBLOCK, LANE), 0)
    lane = lax.broadcasted_iota(I32, (QBLOCK, LANE), 1)
    qpos = i * QBLOCK + row

    qi = qi_ref[0]
    qis = jnp.concatenate([qi[:, h * D_IDX:(h + 1) * D_IDX] for h in range(H_IDX)], axis=0)
    wi = kiwiq_ref[0]
    for h in range(H_IDX):
        wb_scr[h] = jnp.broadcast_to(wi[:, D_IDX + h:D_IDX + h + 1], (QBLOCK, LANE))

    def score_tile(j, carry):
        ki = kiwi_ref[0, pl.ds(j * LANE, LANE), :][:, :D_IDX]
        s = _mm_nt(qis, ki)
        acc = jnp.zeros((QBLOCK, LANE), F32)
        for h in range(H_IDX):
            acc = acc + jnp.maximum(s[h * QBLOCK:(h + 1) * QBLOCK], 0.0) * wb_scr[h]
        acc = jnp.where(j * LANE + lane <= qpos, acc, -jnp.inf)
        key_scr[j] = _sortable(acc)
        return carry

    lax.fori_loop(0, nt, score_tile, 0)

    def count(pred):
        def body(j, acc):
            return acc + pred(j, key_scr[j]).astype(I32)
        acc = lax.fori_loop(0, nt, body, jnp.zeros((QBLOCK, LANE), I32))
        return jnp.sum(acc.astype(F32), axis=1, keepdims=True)

    count_ge = lambda cand: count(lambda j, x: x >= cand)
    thr = _kth_threshold(count_ge, topk, (QBLOCK, 1))
    n_gt = count(lambda j, x: x > thr)
    n_eq = count(lambda j, x: x == thr)
    need = topk - n_gt
    neg_inf_key = jnp.int32(-2139095041)
    tie = jnp.max(((n_eq > need) & (thr > neg_inf_key)).astype(I32)) > 0

    def tie_search(_):
        n_bits = 13

        def body(it, c):
            cand = c + lax.shift_left(jnp.int32(1), n_bits - 1 - it)
            below = count(lambda j, x: (x == thr) & (j * LANE + lane < cand))
            return jnp.where(below < need, cand, c)

        return lax.fori_loop(0, n_bits, body, jnp.zeros((QBLOCK, 1), I32))

    cut = lax.cond(tie, tie_search, lambda _: jnp.full((QBLOCK, 1), 1 << 20, I32), 0)

    def bias_tile(j, carry):
        x = key_scr[j]
        kpos = j * LANE + lane
        sel = ((x > thr) | ((x == thr) & (kpos <= cut))) & (kpos <= qpos)
        bias_scr[j] = jnp.where(sel, 0.0, NEG_BIG)
        return carry

    lax.fori_loop(0, nt, bias_tile, 0)

    q = q_ref[0]
    outs = []
    for h in range(H_B):
        hs = slice(h * HD_B, (h + 1) * HD_B)
        qh = q[:, hs]

        def attend(j, carry, hs=hs, qh=qh):
            m, l, acc = carry
            kt = k_ref[0, pl.ds(j * LANE, LANE), hs]
            vt = v_ref[0, pl.ds(j * LANE, LANE), hs]
            s = _mm_nt(qh, kt) + bias_scr[j]
            m_new = jnp.maximum(m, jnp.max(s, axis=1, keepdims=True))
            p = jnp.exp(s - m_new)
            alpha = jnp.exp(m - m_new)
            l = alpha * l + jnp.sum(p, axis=1, keepdims=True)
            acc = alpha * acc + _mm(p, vt)
            return m_new, l, acc

        m, l, acc = lax.fori_loop(
            0, nt, attend,
            (jnp.full((QBLOCK, 1), NEG_BIG, F32), jnp.zeros((QBLOCK, 1), F32), jnp.zeros((QBLOCK, HD_B), F32)))
        outs.append(acc / l)
    o_ref[0] = jnp.concatenate(outs, axis=1)


def _dsa(q, qi, kiwi, k, v):
    b, t, _ = q.shape
    nq = t // QBLOCK
    topk = min(TOPK_MAX, t // 4)
    blk = lambda w: pl.BlockSpec((1, QBLOCK, w), lambda bi, i: (bi, i, 0))
    seq = lambda w: pl.BlockSpec((1, t, w), lambda bi, i: (bi, 0, 0))
    return pl.pallas_call(
        functools.partial(_dsa_kernel, topk=topk),
        grid=(b, nq),
        in_specs=[blk(D_B), blk(D_B), blk(LANE), seq(D_B), seq(D_B), seq(LANE)],
        out_specs=blk(D_B),
        out_shape=jax.ShapeDtypeStruct((b, t, D_B), F32),
        scratch_shapes=[pltpu.VMEM((nq, QBLOCK, LANE), I32),
                        pltpu.VMEM((nq, QBLOCK, LANE), F32),
                        pltpu.VMEM((H_IDX, QBLOCK, LANE), F32)],
        compiler_params=_cparams(("parallel", "arbitrary")),
        name="dsa",
    )(q, qi, kiwi, k, v, kiwi)


PAGES_PER_STEP = 8


def _head_rows(x):
    full = jnp.broadcast_to(x, (H_B, x.shape[-1]))
    r = lax.broadcasted_iota(I32, full.shape, 0)
    c = lax.broadcasted_iota(I32, full.shape, 1)
    return jnp.where((c >> 6) == r, full, 0.0)


def _pidx_kernel(pt_ref, qi_ref, wcol_ref, *rest):
    del pt_ref
    page_refs, o_ref = rest[:PAGES_PER_STEP], rest[PAGES_PER_STEP]
    qi = qi_ref[0].astype(F32)
    qis = jnp.concatenate([qi[:, h * D_IDX:(h + 1) * D_IDX] for h in range(H_IDX)], axis=0)
    w = wcol_ref[0]
    for u in range(PAGES_PER_STEP):
        s = _mm_nt(qis, page_refs[u][0, 0])
        o_ref[0, 0, u:u + 1, :] = jnp.sum(jnp.maximum(s, 0.0) * w, axis=0, keepdims=True)


def _paged_scores(page_table, qi, wcol, cache_kidx, layer):
    db, n_pages = page_table.shape
    steps = n_pages // PAGES_PER_STEP

    def page_spec(u):
        return pl.BlockSpec((1, 1, PAGE_SIZE, D_IDX),
                            lambda b, j, pt, u=u: (layer, pt[b, j * PAGES_PER_STEP + u], 0, 0))

    return pl.pallas_call(
        _pidx_kernel,
        grid_spec=pltpu.PrefetchScalarGridSpec(
            num_scalar_prefetch=1,
            grid=(db, steps),
            in_specs=[pl.BlockSpec((1, 1, D_B), lambda b, j, pt: (b, 0, 0)),
                      pl.BlockSpec((1, H_IDX, 1), lambda b, j, pt: (b, 0, 0))]
                     + [page_spec(u) for u in range(PAGES_PER_STEP)],
            out_specs=pl.BlockSpec((1, 1, PAGES_PER_STEP, PAGE_SIZE), lambda b, j, pt: (b, j, 0, 0)),
        ),
        out_shape=jax.ShapeDtypeStruct((db, steps, PAGES_PER_STEP, PAGE_SIZE), F32),
        compiler_params=_cparams(("parallel", "arbitrary")),
        name="paged_scores",
    )(page_table, qi, wcol, *([cache_kidx] * PAGES_PER_STEP))


def _psel_kernel(s_ref, qi_ref, kiwi_ref, bias_ref, biasn_ref, *, topk):
    db, n_past = s_ref.shape
    past = _sortable(s_ref[...])
    qi = qi_ref[...].astype(F32)
    kiwi = kiwi_ref[...]
    ki = kiwi[:, :D_IDX]
    s_new = jnp.zeros((db, 1), F32)
    for h in range(H_IDX):
        d = jnp.sum(qi[:, h * D_IDX:(h + 1) * D_IDX].astype(BF16).astype(F32) * ki.astype(BF16).astype(F32),
                    axis=1, keepdims=True)
        s_new = s_new + jnp.maximum(d, 0.0) * kiwi[:, D_IDX + h:D_IDX + h + 1]
    new = _sortable(s_new)

    def count(pred_past, pred_new):
        return jnp.sum(pred_past.astype(F32), axis=1, keepdims=True) + pred_new.astype(F32)

    count_ge = lambda cand: count(past >= cand, new >= cand)
    thr = _kth_threshold(count_ge, topk, (db, 1))
    n_gt = count(past > thr, new > thr)
    need = topk - n_gt
    kpos = lax.broadcasted_iota(I32, (db, n_past), 1)
    eq = past == thr

    def body(it, c):
        cand = c + lax.shift_left(jnp.int32(1), 15 - it)
        below = jnp.sum((eq & (kpos < cand)).astype(F32), axis=1, keepdims=True)
        return jnp.where(below < need, cand, c)

    cut = lax.fori_loop(0, 16, body, jnp.zeros((db, 1), I32))
    sel = (past > thr) | (eq & (kpos <= cut))
    bias_ref[...] = jnp.where(sel, 0.0, NEG_BIG)
    n_eq_past = jnp.sum(eq.astype(F32), axis=1, keepdims=True)
    sel_new = (new > thr) | ((new == thr) & (n_eq_past < need))
    biasn_ref[...] = jnp.broadcast_to(jnp.where(sel_new, 0.0, NEG_BIG), biasn_ref.shape)


def _paged_select(scores, qi, kiwi, topk):
    db, n_past = scores.shape
    return pl.pallas_call(
        functools.partial(_psel_kernel, topk=topk),
        out_shape=[jax.ShapeDtypeStruct((db, n_past), F32), jax.ShapeDtypeStruct((db, LANE), F32)],
        compiler_params=pltpu.CompilerParams(vmem_limit_bytes=VMEM_LIMIT),
        name="paged_select",
    )(scores, qi, kiwi)


def _pattn_kernel(pt_ref, q_ref, bias_ref, biasn_ref, kn_ref, vn_ref, *rest, steps):
    del pt_ref
    k_refs = rest[:PAGES_PER_STEP]
    v_refs = rest[PAGES_PER_STEP:2 * PAGES_PER_STEP]
    o_ref, m_scr, l_scr, acc_scr = rest[2 * PAGES_PER_STEP:]
    j = pl.program_id(1)
    qrows = _head_rows(q_ref[0].astype(F32))

    @pl.when(j == 0)
    def _():
        m_scr[...] = jnp.full(m_scr.shape, NEG_BIG, F32)
        l_scr[...] = jnp.zeros(l_scr.shape, F32)
        acc_scr[...] = jnp.zeros(acc_scr.shape, F32)

    def update(s, vt):
        m = m_scr[...]
        m_new = jnp.maximum(m, jnp.max(s, axis=1, keepdims=True))
        p = jnp.exp(s - m_new)
        alpha = jnp.exp(m - m_new)
        l_scr[...] = alpha * l_scr[...] + jnp.sum(p, axis=1, keepdims=True)
        acc_scr[...] = alpha * acc_scr[...] + _mm(p, vt)
        m_scr[...] = m_new

    for u in range(PAGES_PER_STEP):
        s = _mm_nt(qrows, k_refs[u][0, 0]) + bias_ref[0, 0, u:u + 1, :]
        update(s, v_refs[u][0, 0])

    @pl.when(j == steps - 1)
    def _():
        kn = jnp.broadcast_to(kn_ref[0], (LANE, D_B))
        vn = jnp.broadcast_to(vn_ref[0], (LANE, D_B))
        lane = lax.broadcasted_iota(I32, (H_B, LANE), 1)
        s = _mm_nt(qrows, kn) + jnp.where(lane == 0, biasn_ref[0], NEG_BIG)
        update(s, vn)
        full = acc_scr[...] / l_scr[...]
        r = lax.broadcasted_iota(I32, full.shape, 0)
        c = lax.broadcasted_iota(I32, full.shape, 1)
        o_ref[0] = jnp.sum(jnp.where((c >> 6) == r, full, 0.0), axis=0, keepdims=True)


def _paged_attend(page_table, q, bias, bias_new, k_new, v_new, cache_k, cache_v, layer):
    db, n_pages = page_table.shape
    steps = n_pages // PAGES_PER_STEP

    def page_spec(u):
        return pl.BlockSpec((1, 1, PAGE_SIZE, D_B),
                            lambda b, j, pt, u=u: (layer, pt[b, j * PAGES_PER_STEP + u], 0, 0))

    vec = lambda w: pl.BlockSpec((1, 1, w), lambda b, j, pt: (b, 0, 0))
    return pl.pallas_call(
        functools.partial(_pattn_kernel, steps=steps),
        grid_spec=pltpu.PrefetchScalarGridSpec(
            num_scalar_prefetch=1,
            grid=(db, steps),
            in_specs=[vec(D_B),
                      pl.BlockSpec((1, 1, PAGES_PER_STEP, PAGE_SIZE), lambda b, j, pt: (b, j, 0, 0)),
                      vec(LANE), vec(D_B), vec(D_B)]
                     + [page_spec(u) for u in range(PAGES_PER_STEP)] * 2,
            out_specs=vec(D_B),
            scratch_shapes=[pltpu.VMEM((H_B, 1), F32), pltpu.VMEM((H_B, 1), F32), pltpu.VMEM((H_B, D_B), F32)],
        ),
        out_shape=jax.ShapeDtypeStruct((db, 1, D_B), F32),
        compiler_params=_cparams(("parallel", "arbitrary")),
        name="paged_attend",
    )(page_table, q, bias, bias_new, k_new, v_new,
      *([cache_k] * PAGES_PER_STEP), *([cache_v] * PAGES_PER_STEP))


def _rwkv_pre(pc, prev, mu, w0, w2, a0, a2, g2, ckk, cka, seg):
    xm = pc + (prev - pc) * mu
    r = xm[:, 0:D_C]
    k = xm[:, D_C:2 * D_C]
    v = xm[:, 2 * D_C:3 * D_C]
    o = 3 * D_C
    lw = xm[:, o:o + LORA_W]
    la = xm[:, o + LORA_W:o + LORA_W + LORA_A]
    lg = xm[:, o + LORA_W + LORA_A:]
    z = -(w0 + _mm(jnp.tanh(lw), w2))
    softplus = jnp.maximum(z, 0.0) + jnp.log(1.0 + jnp.exp(-jnp.abs(z)))
    log_decay = -jnp.exp(-softplus - 0.5)
    a = _sigmoid(a0 + _mm(la, a2))
    g = _mm(_sigmoid(lg), g2)
    kk = k * ckk
    kk = kk * lax.rsqrt(jnp.maximum(_mm_exact_lhs(kk * kk, seg), 1e-24))
    kmod = k * (1.0 + (a - 1.0) * cka)
    return r, log_decay, kmod, v, kk, a, g


def _rwkv_post(y, r, kmod, v, g, rk, lnx_g, lnx_b, seg):
    mu = _mm_exact_lhs(y, seg) * (1.0 / N_C)
    yc = y - mu
    var = _mm_exact_lhs(yc * yc, seg) * (1.0 / N_C)
    y = yc * lax.rsqrt(var + LNX_EPS) * lnx_g + lnx_b
    y = y + _mm_exact_lhs(r * kmod * rk, seg) * v
    return y * g


def _rwkv_kernel(pc_ref, prev_ref, mu_ref, w0_ref, w2_ref, a0_ref, a2_ref, g2_ref, ckk_ref, cka_ref,
                 rk_ref, lng_ref, lnb_ref, seg_ref, tril_ref,
                 y_ref, s_ref, h_scr, *, n_chunks):
    c_len = SCAN_CHUNK
    ti = pl.program_id(1)

    @pl.when(ti == 0)
    def _():
        h_scr[...] = jnp.zeros(h_scr.shape, F32)

    seg = seg_ref[...]
    tril = tril_ref[...]
    rr = lax.broadcasted_iota(I32, (c_len, c_len), 0)
    cc = lax.broadcasted_iota(I32, (c_len, c_len), 1)
    strict = rr > cc
    incl = rr >= cc
    eye = (rr == cc).astype(F32)

    def chunk(c, carry):
        rows = pl.ds(pl.multiple_of(c * c_len, c_len), c_len)
        r, logw, kmod, v, kk, a, g = _rwkv_pre(
            pc_ref[0, rows, :], prev_ref[0, rows, :], mu_ref[...], w0_ref[...], w2_ref[...], a0_ref[...],
            a2_ref[...], g2_ref[...], ckk_ref[...], cka_ref[...], seg)
        cum = _mm_exact_rhs(tril, logw)
        p_incl = jnp.exp(cum)
        p_excl = jnp.exp(cum - logw)
        inv = jnp.exp(-cum)
        at = -kk * p_excl
        rt = r * p_incl
        bt = kk * a * inv
        kt = kmod * inv
        p_end = p_incl[c_len - 1:c_len, :]
        ys = []
        for h in range(H_C):
            hs = slice(h * N_C, (h + 1) * N_C)
            ar = jnp.concatenate([at[:, hs], rt[:, hs]], axis=0)
            gb = _mm_nt(ar, bt[:, hs])
            gk = _mm_nt(ar, kt[:, hs])
            lmat = jnp.where(strict, gb[:c_len], 0.0)
            akm = jnp.where(strict, gk[:c_len], 0.0)
            rbm = jnp.where(incl, gb[c_len:], 0.0)
            rkm = jnp.where(incl, gk[c_len:], 0.0)
            tinv = eye + lmat
            pw = lmat
            for _ in range(5):
                pw = _mm(pw, pw)
                tinv = tinv + _mm(tinv, pw)
            s0 = h_scr[h]
            vh = v[:, hs]
            u = _mm(tinv, _mm_nt(at[:, hs], s0) + _mm(akm, vh))
            ys.append(_mm_nt(rt[:, hs], s0) + _mm(rbm, u) + _mm(rkm, vh))
            pe = p_end[:, hs]
            h_scr[h] = s0 * pe + _mm_tn(u, bt[:, hs] * pe) + _mm_tn(vh, kt[:, hs] * pe)
        y = jnp.concatenate(ys, axis=1)
        y_ref[0, rows, :] = _rwkv_post(y, r, kmod, v, g, rk_ref[...], lng_ref[...], lnb_ref[...], seg)
        return carry

    lax.fori_loop(0, n_chunks, chunk, 0)

    @pl.when(ti == pl.num_programs(1) - 1)
    def _():
        s_ref[0] = h_scr[...]


def _rwkv(pc_arr, prev, cp, tm):
    b, t, _ = prev.shape
    n_chunks = tm // SCAN_CHUNK
    const = lambda a: pl.BlockSpec(a.shape, lambda bi, i: (0,) * a.ndim)
    consts = [cp[n] for n in ("mu", "w0", "w2", "a0", "a2", "g2", "ckk", "cka", "rk", "lng", "lnb", "seg", "tril")]
    return pl.pallas_call(
        functools.partial(_rwkv_kernel, n_chunks=n_chunks),
        grid=(b, t // tm),
        in_specs=[pl.BlockSpec((1, tm, C_IN), lambda bi, i: (bi, i, OFF_C // C_IN)),
                  pl.BlockSpec((1, tm, C_IN), lambda bi, i: (bi, i, 0))] + [const(a) for a in consts],
        out_specs=[pl.BlockSpec((1, tm, D_C), lambda bi, i: (bi, i, 0)),
                   pl.BlockSpec((1, H_C, N_C, N_C), lambda bi, i: (bi, 0, 0, 0))],
        out_shape=[jax.ShapeDtypeStruct((b, t, D_C), F32), jax.ShapeDtypeStruct((b, H_C, N_C, N_C), F32)],
        scratch_shapes=[pltpu.VMEM((H_C, N_C, N_C), F32)],
        compiler_params=_cparams(("parallel", "arbitrary")),
        name="rwkv",
    )(pc_arr, prev, *consts)


def _rwkv_pre_kernel(pc_ref, prev_ref, mu_ref, w0_ref, w2_ref, a0_ref, a2_ref, g2_ref, ckk_ref, cka_ref,
                     rk_ref, seg_ref, r_ref, w_ref, k_ref, v_ref, a_ref, b_ref, g_ref, rkv_ref):
    seg = seg_ref[...]
    r, logw, kmod, v, kk, a, g = _rwkv_pre(
        pc_ref[...], prev_ref[...], mu_ref[...], w0_ref[...], w2_ref[...], a0_ref[...], a2_ref[...],
        g2_ref[...], ckk_ref[...], cka_ref[...], seg)
    r_ref[...] = r
    w_ref[...] = jnp.exp(logw)
    k_ref[...] = kmod
    v_ref[...] = v
    a_ref[...] = -kk
    b_ref[...] = kk * a
    g_ref[...] = g
    rkv_ref[...] = _mm_exact_lhs(r * kmod * rk_ref[...], seg) * v


def _rwkv_pre_call(p, prev, cp):
    m = prev.shape[0]
    consts = [cp[n] for n in ("mu", "w0", "w2", "a0", "a2", "g2", "ckk", "cka", "rk", "seg")]
    const = lambda a: pl.BlockSpec(a.shape, lambda i: (0,) * a.ndim)
    return pl.pallas_call(
        _rwkv_pre_kernel,
        grid=(1,),
        in_specs=[pl.BlockSpec((m, C_IN), lambda i: (0, OFF_C // C_IN)),
                  pl.BlockSpec((m, C_IN), lambda i: (0, 0))] + [const(a) for a in consts],
        out_specs=[pl.BlockSpec((m, D_C), lambda i: (0, 0))] * 8,
        out_shape=[jax.ShapeDtypeStruct((m, D_C), F32)] * 8,
        name="rwkv_pre",
    )(p, prev, *consts)


def _rwkv_step_kernel(s_ref, r_ref, w_ref, k_ref, a_ref, b_ref, v_ref, g_ref, rkv_ref, lng_ref, lnb_ref,
                      y_ref, so_ref):
    s = s_ref[0]
    sa = jnp.sum(s * a_ref[0], axis=-1, keepdims=True)
    v = v_ref[0]
    s = s * w_ref[0] + sa * b_ref[0] + v * k_ref[0]
    so_ref[0] = s
    y = jnp.sum(s * r_ref[0], axis=-1, keepdims=True)
    mu = jnp.mean(y, axis=1, keepdims=True)
    yc = y - mu
    var = jnp.mean(yc * yc, axis=1, keepdims=True)
    y = yc * lax.rsqrt(var + LNX_EPS) * lng_ref[...] + lnb_ref[...]
    y_ref[0] = (y + rkv_ref[0]) * g_ref[0]


def _rwkv_step(state, rows, cols, lng_col, lnb_col):
    db = state.shape[0]
    rspec = pl.BlockSpec((1, H_C, 1, N_C), lambda b: (b, 0, 0, 0))
    cspec = pl.BlockSpec((1, H_C, N_C, 1), lambda b: (b, 0, 0, 0))
    pspec = pl.BlockSpec((H_C, N_C, 1), lambda b: (0, 0, 0))
    sspec = pl.BlockSpec((1, H_C, N_C, N_C), lambda b: (b, 0, 0, 0))
    return pl.pallas_call(
        _rwkv_step_kernel,
        grid=(db,),
        in_specs=[sspec] + [rspec] * 5 + [cspec] * 3 + [pspec] * 2,
        out_specs=[cspec, sspec],
        out_shape=[jax.ShapeDtypeStruct((db, H_C, N_C, 1), F32), jax.ShapeDtypeStruct(state.shape, F32)],
        compiler_params=_cparams(("parallel",)),
        name="rwkv_step",
    )(state, *rows, *cols, lng_col, lnb_col)


def _merge_kernel(x_ref, ya_ref, yb_ref, yc_ref, ga_ref, gb_ref, gc_ref, wa_ref, wb_ref, wc_ref, wo_ref, g_ref,
                  o_ref):
    merged = (_sigmoid(ga_ref[...]) * _mm(ya_ref[...], wa_ref[...])
              + _sigmoid(gb_ref[...]) * _mm(yb_ref[...], wb_ref[...])
              + _sigmoid(gc_ref[...]) * _mm(yc_ref[...], wc_ref[...]))
    o_ref[...] = x_ref[...] + _rms(_mm(merged, wo_ref[...]), g_ref[...])


def _merge(x, ya, yb, yc, p, wa, wb, wc, wo, g, tm):
    m = x.shape[0]
    row = lambda w: pl.BlockSpec((tm, w), lambda i: (i, 0))
    gate = lambda n: pl.BlockSpec((tm, D_MODEL), lambda i, n=n: (i, OFF_GATE // D_MODEL + n))
    const = lambda a: pl.BlockSpec(a.shape, lambda i: (0,) * a.ndim)
    return pl.pallas_call(
        _merge_kernel,
        grid=(m // tm,),
        in_specs=[row(D_MODEL), row(D_A), row(D_B), row(D_C), gate(0), gate(1), gate(2),
                  const(wa), const(wb), const(wc), const(wo), const(g)],
        out_specs=row(D_MODEL),
        out_shape=jax.ShapeDtypeStruct((m, D_MODEL), F32),
        compiler_params=_cparams(("parallel",)),
        name="merge",
    )(x, ya, yb, yc, p, p, p, wa, wb, wc, wo, g)


def _ffn_kernel(x_ref, gpre_ref, w1_ref, w3_ref, w2_ref, gpost_ref, o_ref, h_scr, acc_scr):
    j = pl.program_id(1)

    @pl.when(j == 0)
    def _():
        h_scr[...] = _rms(x_ref[...], gpre_ref[...]).astype(BF16)
        acc_scr[...] = jnp.zeros(acc_scr.shape, F32)

    h = h_scr[...]
    f1 = jnp.dot(h, w1_ref[...], preferred_element_type=F32)
    f3 = jnp.dot(h, w3_ref[...], preferred_element_type=F32)
    acc_scr[...] += _mm(f1 * _sigmoid(f1) * f3, w2_ref[...])

    @pl.when(j == pl.num_programs(1) - 1)
    def _():
        o_ref[...] = x_ref[...] + _rms(acc_scr[...], gpost_ref[...])


def _ffn(x, gpre, w_in, w_out, gpost, tm, nf):
    m = x.shape[0]
    tf = D_FF // nf
    return pl.pallas_call(
        _ffn_kernel,
        grid=(m // tm, nf),
        in_specs=[pl.BlockSpec((tm, D_MODEL), lambda i, j: (i, 0)),
                  pl.BlockSpec((1, D_MODEL), lambda i, j: (0, 0)),
                  pl.BlockSpec((D_MODEL, tf), lambda i, j: (0, j)),
                  pl.BlockSpec((D_MODEL, tf), lambda i, j: (0, nf + j)),
                  pl.BlockSpec((tf, D_MODEL), lambda i, j: (j, 0)),
                  pl.BlockSpec((1, D_MODEL), lambda i, j: (0, 0))],
        out_specs=pl.BlockSpec((tm, D_MODEL), lambda i, j: (i, 0)),
        out_shape=jax.ShapeDtypeStruct((m, D_MODEL), F32),
        scratch_shapes=[pltpu.VMEM((tm, D_MODEL), BF16), pltpu.VMEM((tm, D_MODEL), F32)],
        compiler_params=_cparams(("parallel", "arbitrary")),
        name="ffn",
    )(x, gpre, w_in, w_in, w_out, gpost)


def _pad_w_in(w):
    s = {}
    o = 0
    for name, width in (("au", D_A), ("av", D_A), ("q", D_B), ("k", D_B), ("v", D_B), ("qi", H_IDX * D_IDX),
                        ("ki", D_IDX), ("wi", H_IDX), ("c", C_IN), ("gate", 3 * D_MODEL)):
        s[name] = w[:, o:o + width]
        o += width
    z = lambda n: jnp.zeros((w.shape[0], n), w.dtype)
    cols = [s["c"], z(OFF_GATE - C_IN), s["gate"], s["au"], s["av"], s["q"], s["k"], s["v"], s["qi"],
            s["ki"], s["wi"], z(LANE - D_IDX - H_IDX)]
    return jnp.concatenate(cols, axis=1).astype(BF16)


def _rope_tables(pos):
    half = HD_B // 2
    inv_freq = ROPE_THETA ** (-jnp.arange(half, dtype=F32) / half)
    ang = pos.astype(F32)[:, None] * inv_freq[None, :]
    cos = jnp.cos(ang)
    sin = jnp.sin(ang)
    return jnp.concatenate([cos] * 4, axis=1), jnp.concatenate([-sin, sin] * 2, axis=1)


def _layer_params(l, P):
    row = lambda a: a[l].reshape(1, -1)
    eye_h = jnp.repeat(jnp.repeat(jnp.eye(H_C, dtype=F32), N_C, axis=0), N_C, axis=1)
    causal = jnp.tril(jnp.ones((CHUNK, CHUNK), dtype=bool))
    ws_tril = jnp.where(causal[None], P["a_ws"][l], 0.0)
    cp = dict(mu=row(P["c_mu"]), w0=row(P["c_w0"]), w2=P["c_w2"][l].astype(BF16), a0=row(P["c_a0"]),
              a2=P["c_a2"][l].astype(BF16), g2=P["c_g2"][l].astype(BF16), ckk=row(P["c_kk"]), cka=row(P["c_ka"]),
              rk=row(P["c_rk"]), lng=row(P["c_lnx_g"]), lnb=row(P["c_lnx_b"]), seg=eye_h.astype(BF16),
              tril=jnp.tril(jnp.ones((SCAN_CHUNK, SCAN_CHUNK), F32)).astype(BF16))
    return dict(
        g_pre_mix=row(P["g_pre_mix"]), g_post_mix=row(P["g_post_mix"]),
        g_pre_ffn=row(P["g_pre_ffn"]), g_post_ffn=row(P["g_post_ffn"]),
        w_in=_pad_w_in(P["w_in"][l]),
        a_ln_g=row(P["a_ln_g"]), a_ln_b=row(P["a_ln_b"]),
        ws_tril=ws_tril.astype(BF16),
        bs_full=jnp.repeat(P["a_bs"][l].T, D_A // G_A, axis=1),
        ws00=jnp.repeat(P["a_ws"][l][:, 0, 0], D_A // G_A).reshape(1, -1),
        bs0=jnp.repeat(P["a_bs"][l][:, 0], D_A // G_A).reshape(1, -1),
        cp=cp,
        lng_col=P["c_lnx_g"][l].reshape(H_C, N_C, 1), lnb_col=P["c_lnx_b"][l].reshape(H_C, N_C, 1),
        w_br_a=P["w_br_a"][l].astype(BF16), w_br_b=P["w_br_b"][l].astype(BF16), w_br_c=P["w_br_c"][l].astype(BF16),
        w_out=P["w_out"][l].astype(BF16), w_ffn_in=P["w_ffn_in"][l].astype(BF16),
        w_ffn_out=P["w_ffn_out"][l].astype(BF16))


def _pick(n, prefs):
    for c in prefs:
        if n % c == 0:
            return c
    return n


def _finish_layer(x2d, ya, yb, yc, p2d, L, tm):
    x1 = _merge(x2d, ya, yb, yc, p2d, L["w_br_a"], L["w_br_b"], L["w_br_c"], L["w_out"], L["g_post_mix"], tm)
    return _ffn(x1, L["g_pre_ffn"], L["w_ffn_in"], L["w_ffn_out"], L["g_post_ffn"], tm, 2)


def _prompt_layer(x, L, cos, sin):
    b, t, _ = x.shape
    m = b * t
    x2d = x.reshape(m, D_MODEL)
    tm = _pick(m, (1024, 512, 256, 128))
    p2d = _proj(x2d, L["g_pre_mix"], L["w_in"], tm, N_PAD // 5)
    p = p2d.reshape(b, t, N_PAD)
    ya = _sgu(p, L["a_ln_g"], L["a_ln_b"], L["ws_tril"], L["bs_full"], _pick(t // CHUNK, (4, 2, 1)))
    q, k, kb, vb, qi, kiwi = _rope(p, cos, sin, _pick(t, (512, 256, 128)))
    yb = _dsa(q, qi, kiwi, kb, vb)
    pc = p[:, :, OFF_C:OFF_C + C_IN]
    prev = jnp.concatenate([jnp.zeros((b, 1, C_IN), F32), pc[:, :-1]], axis=1)
    yc, wkv = _rwkv(p, prev, L["cp"], _pick(t, (512, 256, 128, 64)))
    y = _finish_layer(x2d, ya.reshape(m, D_A), yb.reshape(m, D_B), yc.reshape(m, D_C), p2d, L,
                      _pick(m, (512, 256, 128)))
    v = p[:, :, OFF_V:OFF_V + D_B]
    return (y.reshape(b, t, D_MODEL), k.reshape(b, t, H_B, HD_B), v.reshape(b, t, H_B, HD_B),
            kiwi[:, :, :D_IDX], pc[:, -1], wkv)


def _sample_layer(x, L, cos, sin, layer, cache_k, cache_v, cache_kidx, page_table, c_prev, wkv0):
    db, t, _ = x.shape
    x2d = x.reshape(db, D_MODEL)
    p2d = _proj(x2d, L["g_pre_mix"], L["w_in"], db, N_PAD // 5)
    ya, va = _sgu_step(p2d, L["a_ln_g"], L["a_ln_b"], L["ws00"], L["bs0"])
    q, k, kb, vb, qi, kiwi = _rope(p2d[None], cos, sin, db)
    n_past = page_table.shape[1] * PAGE_SIZE
    topk = min(TOPK_MAX, (n_past + t) // 4)
    wcol = kiwi[0, :, D_IDX:D_IDX + H_IDX].reshape(db, H_IDX, 1)
    scores = _paged_scores(page_table, qi.reshape(db, 1, D_B), wcol, cache_kidx, layer)
    bias, bias_new = _paged_select(scores.reshape(db, n_past), qi[0], kiwi[0], topk)
    steps = page_table.shape[1] // PAGES_PER_STEP
    yb = _paged_attend(page_table, q.reshape(db, 1, D_B), bias.reshape(db, steps, PAGES_PER_STEP, PAGE_SIZE),
                       bias_new.reshape(db, 1, LANE), kb.reshape(db, 1, D_B), vb.reshape(db, 1, D_B),
                       cache_k.reshape(cache_k.shape[:3] + (D_B,)), cache_v.reshape(cache_v.shape[:3] + (D_B,)),
                       layer)
    r, w, kmod, v_c, a_c, b_c, g, rkv = _rwkv_pre_call(p2d, c_prev, L["cp"])
    rowv = lambda a: a.reshape(db, H_C, 1, N_C)
    colv = lambda a: a.reshape(db, H_C, N_C, 1)
    yc, wkv = _rwkv_step(wkv0, [rowv(r), rowv(w), rowv(kmod), rowv(a_c), rowv(b_c)],
                         [colv(v_c), colv(g), colv(rkv)], L["lng_col"], L["lnb_col"])
    y = _finish_layer(x2d, ya, yb.reshape(db, D_B), yc.reshape(db, D_C), p2d, L, db)
    v = p2d[:, OFF_V:OFF_V + D_B]
    return (y.reshape(db, t, D_MODEL), k.reshape(db, t, H_B, HD_B), v.reshape(db, t, H_B, HD_B),
            kiwi[0, :, :D_IDX].reshape(db, t, D_IDX), va.reshape(db, t, D_A),
            p2d[:, OFF_C:OFF_C + C_IN], wkv)


def kernel(x_prompt, x_sample, cache_k, cache_v, cache_kidx, state_c_shift, state_c_wkv, page_table,
           g_pre_mix, g_post_mix, g_pre_ffn, g_post_ffn, w_in, a_ln_g, a_ln_b, a_ws, a_bs,
           c_mu, c_w0, c_w2, c_a0, c_a2, c_g2, c_kk, c_ka, c_rk, c_lnx_g, c_lnx_b,
           w_br_a, w_br_b, w_br_c, w_out, w_ffn_in, w_ffn_out):
    P = dict(g_pre_mix=g_pre_mix, g_post_mix=g_post_mix, g_pre_ffn=g_pre_ffn, g_post_ffn=g_post_ffn, w_in=w_in,
             a_ln_g=a_ln_g, a_ln_b=a_ln_b, a_ws=a_ws, a_bs=a_bs, c_mu=c_mu, c_w0=c_w0, c_w2=c_w2, c_a0=c_a0,
             c_a2=c_a2, c_g2=c_g2, c_kk=c_kk, c_ka=c_ka, c_rk=c_rk, c_lnx_g=c_lnx_g, c_lnx_b=c_lnx_b,
             w_br_a=w_br_a, w_br_b=w_br_b, w_br_c=w_br_c, w_out=w_out, w_ffn_in=w_ffn_in, w_ffn_out=w_ffn_out)
    depth = w_in.shape[0]
    t = x_prompt.shape[1]
    db = x_sample.shape[0]
    n_past = page_table.shape[1] * PAGE_SIZE
    cos_p, sin_p = _rope_tables(jnp.arange(t, dtype=jnp.int32))
    cos_s, sin_s = _rope_tables(jnp.full((db,), n_past, dtype=jnp.int32))
    hp, hs = x_prompt, x_sample
    outs_p, outs_s = [], []
    for l in range(depth):
        L = _layer_params(l, P)
        hp, *rest_p = _prompt_layer(hp, L, cos_p, sin_p)
        outs_p.append(rest_p)
        hs, *rest_s = _sample_layer(hs, L, cos_s, sin_s, l, cache_k, cache_v, cache_kidx, page_table,
                                    state_c_shift[l], state_c_wkv[l])
        outs_s.append(rest_s)
    stack = lambda outs, i: jnp.stack([o[i] for o in outs])
    return (hp, hs,
            stack(outs_p, 0), stack(outs_p, 1), stack(outs_p, 2), stack(outs_p, 3), stack(outs_p, 4),
            stack(outs_s, 0), stack(outs_s, 1), stack(outs_s, 2), stack(outs_s, 3), stack(outs_s, 4),
            stack(outs_s, 5))
```

```python
import functools

import jax
import jax.numpy as jnp
from jax import lax
from jax.experimental import pallas as pl
from jax.experimental.pallas import tpu as pltpu

F32 = jnp.float32
BF16 = jnp.bfloat16
I32 = jnp.int32

D_MODEL = 1024
PAGE_SIZE = 128
CHUNK = 128
D_A = 512
G_A = 8
H_B = 8
HD_B = 64
D_B = 512
H_IDX = 8
D_IDX = 64
TOPK_MAX = 256
QBLOCK = 128
H_C = 8
N_C = 64
D_C = 512
LORA_W = 64
LORA_A = 64
LORA_G = 128
C_IN = 3 * D_C + LORA_W + LORA_A + LORA_G
D_FF = 2816
RMS_EPS = 1e-6
LN_EPS = 1e-5
LNX_EPS = 64e-5
ROPE_THETA = 10000.0

OFF_C = 0
OFF_GATE = 2048
OFF_AU = 5120
OFF_AV = 5632
OFF_Q = 6144
OFF_K = 6656
OFF_V = 7168
OFF_QI = 7680
OFF_KIWI = 8192
N_PAD = 8320

LANE = 128
SCAN_CHUNK = 64
INT_MIN = -2147483648
NEG_INF_KEY = -2139095041
NEG_BIG = -1e30
VMEM_LIMIT = 56 * 1024 * 1024


def _cparams(sem):
    return pltpu.CompilerParams(dimension_semantics=sem, vmem_limit_bytes=VMEM_LIMIT)


def _mm(a, b):
    return jnp.dot(a.astype(BF16), b.astype(BF16), preferred_element_type=F32)


def _mm_nt(a, b):
    return lax.dot_general(a.astype(BF16), b.astype(BF16), (((1,), (1,)), ((), ())),
                           preferred_element_type=F32)


def _mm_tn(a, b):
    return lax.dot_general(a.astype(BF16), b.astype(BF16), (((0,), (0,)), ((), ())),
                           preferred_element_type=F32)


def _split3(x):
    hi = x.astype(BF16)
    r1 = x - hi.astype(F32)
    mid = r1.astype(BF16)
    lo = (r1 - mid.astype(F32)).astype(BF16)
    return hi, mid, lo


def _mm_exact_rhs(a_bf16, x):
    hi, mid, lo = _split3(x)
    dot = functools.partial(jnp.dot, preferred_element_type=F32)
    return dot(a_bf16, hi) + dot(a_bf16, mid) + dot(a_bf16, lo)


def _mm_exact_lhs(x, b_bf16):
    hi, mid, lo = _split3(x)
    dot = functools.partial(jnp.dot, preferred_element_type=F32)
    return dot(hi, b_bf16) + dot(mid, b_bf16) + dot(lo, b_bf16)


def _sigmoid(x):
    return 1.0 / (1.0 + jnp.exp(-x))


def _rms(x, g):
    return x * lax.rsqrt(jnp.mean(x * x, axis=-1, keepdims=True) + RMS_EPS) * g


def _proj_kernel(x_ref, g_ref, w_ref, o_ref, h_scr):
    @pl.when(pl.program_id(1) == 0)
    def _():
        h_scr[...] = _rms(x_ref[...], g_ref[...]).astype(BF16)

    o_ref[...] = jnp.dot(h_scr[...], w_ref[...], preferred_element_type=F32)


def _proj(x, g, w, tm, tn):
    m = x.shape[0]
    n = w.shape[1]
    return pl.pallas_call(
        _proj_kernel,
        grid=(m // tm, n // tn),
        in_specs=[pl.BlockSpec((tm, D_MODEL), lambda i, j: (i, 0)),
                  pl.BlockSpec((1, D_MODEL), lambda i, j: (0, 0)),
                  pl.BlockSpec((D_MODEL, tn), lambda i, j: (0, j))],
        out_specs=pl.BlockSpec((tm, tn), lambda i, j: (i, j)),
        out_shape=jax.ShapeDtypeStruct((m, n), F32),
        scratch_shapes=[pltpu.VMEM((tm, D_MODEL), BF16)],
        compiler_params=_cparams(("parallel", "arbitrary")),
        name="proj",
    )(x, g, w)


def _swap_halves(x):
    n = x.shape[-1]
    left = pltpu.roll(x, n - 32, x.ndim - 1)
    right = pltpu.roll(x, 32, x.ndim - 1)
    lane = lax.broadcasted_iota(I32, x.shape, x.ndim - 1)
    return jnp.where((lane & 32) == 0, left, right)


def _rope_values(q_ref, k_ref, qi_ref, kiwi_ref, cos_ref, sin_ref):
    cos1 = cos_ref[...]
    sin1 = sin_ref[...]
    cos = jnp.concatenate([cos1] * 4, axis=1)
    sin = jnp.concatenate([sin1] * 4, axis=1)

    def rope(x, c, s):
        return x * c + _swap_halves(x) * s

    q = rope(q_ref[0], cos, sin) * (HD_B ** -0.5)
    k = rope(k_ref[0], cos, sin)
    qi = rope(qi_ref[0], cos, sin)
    kiwi = kiwi_ref[0]
    lane = lax.broadcasted_iota(I32, kiwi.shape, 1)
    kiwi = jnp.where(lane < D_IDX, rope(kiwi, cos1, sin1), kiwi * ((D_IDX ** -0.5) * (H_IDX ** -0.5)))
    return q, k, qi, kiwi


def _rope_rows_kernel(q_ref, k_ref, qi_ref, kiwi_ref, cos_ref, sin_ref, qo_ref, ko_ref, qio_ref, kiwio_ref):
    q, k, qi, kiwi = _rope_values(q_ref, k_ref, qi_ref, kiwi_ref, cos_ref, sin_ref)
    qo_ref[0] = q
    ko_ref[0] = k
    qio_ref[0] = qi.astype(BF16)
    kiwio_ref[0] = kiwi


def _rope_tiles_kernel(q_ref, k_ref, v_ref, qi_ref, kiwi_ref, cos_ref, sin_ref,
                       qt_ref, ko_ref, kb_ref, vt_ref, qit_ref, kiwio_ref, kiwit_ref, *, n_sub):
    q, k, qi, kiwi = _rope_values(q_ref, k_ref, qi_ref, kiwi_ref, cos_ref, sin_ref)
    v = v_ref[0]
    ko_ref[0] = k
    kb_ref[0] = k.astype(BF16)
    kiwio_ref[0] = kiwi
    for r in range(n_sub):
        rows = slice(r * QBLOCK, (r + 1) * QBLOCK)
        qt_ref[0, r] = jnp.transpose(q[rows]).astype(BF16)
        qit_ref[0, r] = jnp.transpose(qi[rows]).astype(BF16)
        vt_ref[0, r] = jnp.transpose(v[rows]).astype(BF16)
        kiwit_ref[0, r] = jnp.transpose(kiwi[rows])


def _rope_specs(tm):
    def col(off, w):
        return pl.BlockSpec((1, tm, w), lambda bi, i, o=off // w: (bi, i, o))

    row = lambda w: pl.BlockSpec((1, tm, w), lambda bi, i: (bi, i, 0))
    tab = pl.BlockSpec((tm, LANE), lambda bi, i: (i, 0))
    return col, row, tab


def _rope_rows(p, cos, sin, tm):
    b, t, _ = p.shape
    col, row, tab = _rope_specs(tm)
    return pl.pallas_call(
        _rope_rows_kernel,
        grid=(b, t // tm),
        in_specs=[col(OFF_Q, D_B), col(OFF_K, D_B), col(OFF_QI, D_B), col(OFF_KIWI, LANE), tab, tab],
        out_specs=[row(D_B), row(D_B), row(D_B), row(LANE)],
        out_shape=[jax.ShapeDtypeStruct((b, t, D_B), F32),
                   jax.ShapeDtypeStruct((b, t, D_B), F32),
                   jax.ShapeDtypeStruct((b, t, D_B), BF16),
                   jax.ShapeDtypeStruct((b, t, LANE), F32)],
        compiler_params=_cparams(("parallel", "parallel")),
        name="rope_rows",
    )(p, p, p, p, cos, sin)


def _rope_tiles(p, cos, sin, tm):
    b, t, _ = p.shape
    n_sub = tm // QBLOCK
    col, row, tab = _rope_specs(tm)
    tile = lambda w: pl.BlockSpec((1, n_sub, w, QBLOCK), lambda bi, i: (bi, i, 0, 0))
    nq = t // QBLOCK
    return pl.pallas_call(
        functools.partial(_rope_tiles_kernel, n_sub=n_sub),
        grid=(b, t // tm),
        in_specs=[col(OFF_Q, D_B), col(OFF_K, D_B), col(OFF_V, D_B), col(OFF_QI, D_B), col(OFF_KIWI, LANE),
                  tab, tab],
        out_specs=[tile(D_B), row(D_B), row(D_B), tile(D_B), tile(D_B), row(LANE), tile(LANE)],
        out_shape=[jax.ShapeDtypeStruct((b, nq, D_B, QBLOCK), BF16),
                   jax.ShapeDtypeStruct((b, t, D_B), F32),
                   jax.ShapeDtypeStruct((b, t, D_B), BF16),
                   jax.ShapeDtypeStruct((b, nq, D_B, QBLOCK), BF16),
                   jax.ShapeDtypeStruct((b, nq, D_B, QBLOCK), BF16),
                   jax.ShapeDtypeStruct((b, t, LANE), F32),
                   jax.ShapeDtypeStruct((b, nq, LANE, QBLOCK), F32)],
        compiler_params=_cparams(("parallel", "parallel")),
        name="rope_tiles",
    )(p, p, p, p, p, cos, sin)


def _layer_norm(x, g, b, eps):
    mu = jnp.mean(x, axis=-1, keepdims=True)
    xc = x - mu
    var = jnp.mean(xc * xc, axis=-1, keepdims=True)
    return xc * lax.rsqrt(var + eps) * g + b


def _sgu_kernel(u_ref, v_ref, g_ref, b_ref, ws_ref, bs_ref, o_ref, *, n_chunks):
    v = _layer_norm(v_ref[0], g_ref[...], b_ref[...], LN_EPS).astype(BF16)
    bs = bs_ref[...]
    for c in range(n_chunks):
        rows = slice(c * CHUNK, (c + 1) * CHUNK)
        vc = v[rows]
        mixed = jnp.concatenate(
            [jnp.dot(ws_ref[g], vc[:, g * 64:(g + 1) * 64], preferred_element_type=F32) for g in range(G_A)],
            axis=1)
        o_ref[0, rows, :] = u_ref[0, rows, :] * (mixed + bs)


def _sgu(p, ln_g, ln_b, ws_tril, bs_full, n_chunks):
    b, t, _ = p.shape
    tm = n_chunks * CHUNK
    const = lambda shape: pl.BlockSpec(shape, lambda bi, i: (0,) * len(shape))
    return pl.pallas_call(
        functools.partial(_sgu_kernel, n_chunks=n_chunks),
        grid=(b, t // tm),
        in_specs=[pl.BlockSpec((1, tm, D_A), lambda bi, i: (bi, i, OFF_AU // D_A)),
                  pl.BlockSpec((1, tm, D_A), lambda bi, i: (bi, i, OFF_AV // D_A)),
                  const((1, D_A)), const((1, D_A)), const((G_A, CHUNK, CHUNK)), const((CHUNK, D_A))],
        out_specs=pl.BlockSpec((1, tm, D_A), lambda bi, i: (bi, i, 0)),
        out_shape=jax.ShapeDtypeStruct((b, t, D_A), F32),
        compiler_params=_cparams(("parallel", "parallel")),
        name="sgu",
    )(p, p, ln_g, ln_b, ws_tril, bs_full)


def _sgu_step_kernel(u_ref, v_ref, g_ref, b_ref, w0_ref, b0_ref, y_ref, vo_ref):
    v = _layer_norm(v_ref[...], g_ref[...], b_ref[...], LN_EPS)
    vo_ref[...] = v
    y_ref[...] = u_ref[...] * (v * w0_ref[...] + b0_ref[...])


def _sgu_step(p, ln_g, ln_b, w0, b0):
    m = p.shape[0]
    const = pl.BlockSpec((1, D_A), lambda i: (0, 0))
    return pl.pallas_call(
        _sgu_step_kernel,
        grid=(1,),
        in_specs=[pl.BlockSpec((m, D_A), lambda i: (0, OFF_AU // D_A)),
                  pl.BlockSpec((m, D_A), lambda i: (0, OFF_AV // D_A)),
                  const, const, const, const],
        out_specs=[pl.BlockSpec((m, D_A), lambda i: (0, 0))] * 2,
        out_shape=[jax.ShapeDtypeStruct((m, D_A), F32)] * 2,
        name="sgu_step",
    )(p, p, ln_g, ln_b, w0, b0)


def _sortable(x):
    i = pltpu.bitcast(x + 0.0, I32)
    return i ^ ((i >> 31) & 0x7FFFFFFF)


def _kth_threshold(count_ge, k, shape):
    res = jnp.where(count_ge(jnp.zeros(shape, I32)) >= k, 0, INT_MIN).astype(I32)

    def body(it, res):
        cand = res + lax.shift_left(jnp.int32(1), 30 - it)
        return jnp.where(count_ge(cand) >= k, cand, res)

    return lax.fori_loop(0, 31, body, res)


def _dsa_kernel(qt_ref, qit_ref, kiwit_ref, k_ref, vt_ref, kiwi_ref, o_ref, key_scr, bias_scr, acc_scr, *, topk):
    i = pl.program_id(1)
    nt = i + 1
    n_pairs = (nt + 1) // 2
    sub = lax.broadcasted_iota(I32, (LANE, QBLOCK), 0)
    qpos = i * QBLOCK + lax.broadcasted_iota(I32, (LANE, QBLOCK), 1)

    qit = qit_ref[0, 0]
    wit = kiwit_ref[0, 0][D_IDX:D_IDX + H_IDX, :]

    def score_pair(j2, carry):
        tiles = (2 * j2, 2 * j2 + 1)
        ki = [kiwi_ref[0, pl.ds(j * LANE, LANE), :][:, :D_IDX].astype(BF16) for j in tiles]
        s = [[jnp.dot(ki[e], qit[h * D_IDX:(h + 1) * D_IDX, :], preferred_element_type=F32)
              for h in range(H_IDX)] for e in range(2)]
        for e, j in enumerate(tiles):
            acc = jnp.zeros((LANE, QBLOCK), F32)
            for h in range(H_IDX):
                acc = acc + jnp.maximum(s[e][h], 0.0) * wit[h:h + 1, :]
            acc = jnp.where(j * LANE + sub <= qpos, acc, -jnp.inf)
            key_scr[j] = _sortable(acc)
        return carry

    lax.fori_loop(0, n_pairs, score_pair, 0)

    def count(pred):
        def body(j2, acc):
            j = 2 * j2
            return acc + pred(j, key_scr[j]).astype(I32) + pred(j + 1, key_scr[j + 1]).astype(I32)
        acc = lax.fori_loop(0, n_pairs, body, jnp.zeros((LANE, QBLOCK), I32))
        return jnp.sum(acc.astype(F32), axis=0, keepdims=True)

    count_ge = lambda cand: count(lambda j, x: x >= cand)
    thr = _kth_threshold(count_ge, topk, (1, QBLOCK))
    n_gt = count(lambda j, x: x > thr)
    n_eq = count(lambda j, x: x == thr)
    need = topk - n_gt
    tie = jnp.max(((n_eq > need) & (thr > NEG_INF_KEY)).astype(I32)) > 0

    def tie_search(_):
        n_bits = 13

        def body(it, c):
            cand = c + lax.shift_left(jnp.int32(1), n_bits - 1 - it)
            below = count(lambda j, x: (x == thr) & (j * LANE + sub < cand))
            return jnp.where(below < need, cand, c)

        return lax.fori_loop(0, n_bits, body, jnp.zeros((1, QBLOCK), I32))

    cut = lax.cond(tie, tie_search, lambda _: jnp.full((1, QBLOCK), 1 << 20, I32), 0)

    def bias_tile(j, carry):
        x = key_scr[j]
        kpos = j * LANE + sub
        sel = ((x > thr) | ((x == thr) & (kpos <= cut))) & (kpos <= qpos)
        bias_scr[j] = jnp.where(sel, 0.0, NEG_BIG)
        return carry

    lax.fori_loop(0, 2 * n_pairs, bias_tile, 0)

    qt = qt_ref[0, 0]
    acc_scr[...] = jnp.zeros(acc_scr.shape, F32)
    heads = range(H_B)

    def attend(j2, carry):
        ms, ls = carry
        tiles = (2 * j2, 2 * j2 + 1)
        bias = [bias_scr[j] for j in tiles]
        s = [[jnp.dot(k_ref[0, pl.ds(j * LANE, LANE), pl.ds(h * HD_B, HD_B)], qt[h * HD_B:(h + 1) * HD_B, :],
                      preferred_element_type=F32) + bias[e] for h in heads]
             for e, j in enumerate(tiles)]
        m_new = [jnp.maximum(ms[h], jnp.max(jnp.maximum(s[0][h], s[1][h]), axis=0, keepdims=True)) for h in heads]
        p = [[jnp.exp(s[e][h] - m_new[h]) for h in heads] for e in range(2)]
        alpha = [jnp.exp(ms[h] - m_new[h]) for h in heads]
        l_new = [alpha[h] * ls[h] + jnp.sum(p[0][h] + p[1][h], axis=0, keepdims=True) for h in heads]
        pv = [sum(jnp.dot(vt_ref[0, j, h * HD_B:(h + 1) * HD_B, :], p[e][h].astype(BF16),
                          preferred_element_type=F32) for e, j in enumerate(tiles))
              for h in heads]
        for h in heads:
            hs = slice(h * HD_B, (h + 1) * HD_B)
            acc_scr[hs, :] = alpha[h] * acc_scr[hs, :] + pv[h]
        return tuple(m_new), tuple(l_new)

    init = (tuple(jnp.full((1, QBLOCK), NEG_BIG, F32) for _ in heads),
            tuple(jnp.zeros((1, QBLOCK), F32) for _ in heads))
    _, ls = lax.fori_loop(0, n_pairs, attend, init)
    inv = jnp.concatenate([jnp.broadcast_to(1.0 / ls[h], (HD_B, QBLOCK)) for h in heads], axis=0)
    o_ref[0] = jnp.transpose(acc_scr[...] * inv)


def _dsa(qt, qit, kiwit, kiwi, kb, vt):
    b, nq, _, _ = qt.shape
    assert nq % 2 == 0
    t = nq * QBLOCK
    topk = min(TOPK_MAX, t // 4)
    tile = lambda w: pl.BlockSpec((1, 1, w, QBLOCK), lambda bi, i: (bi, i, 0, 0))
    seq = lambda w: pl.BlockSpec((1, t, w), lambda bi, i: (bi, 0, 0))
    return pl.pallas_call(
        functools.partial(_dsa_kernel, topk=topk),
        grid=(b, nq),
        in_specs=[tile(D_B), tile(D_B), tile(LANE), seq(D_B),
                  pl.BlockSpec((1, nq, D_B, QBLOCK), lambda bi, i: (bi, 0, 0, 0)), seq(LANE)],
        out_specs=pl.BlockSpec((1, QBLOCK, D_B), lambda bi, i: (bi, i, 0)),
        out_shape=jax.ShapeDtypeStruct((b, t, D_B), F32),
        scratch_shapes=[pltpu.VMEM((nq, LANE, QBLOCK), I32),
                        pltpu.VMEM((nq, LANE, QBLOCK), F32),
                        pltpu.VMEM((D_B, QBLOCK), F32)],
        compiler_params=_cparams(("parallel", "arbitrary")),
        name="dsa",
    )(qt, qit, kiwit, kb, vt, kiwi)


def _pidx_kernel(pt_ref, qi_ref, wcol_ref, *rest, pps):
    del pt_ref
    page_refs, o_ref = rest[:pps], rest[pps]
    qi = qi_ref[0].astype(F32)
    qis = jnp.concatenate([qi[:, h * D_IDX:(h + 1) * D_IDX] for h in range(H_IDX)], axis=0)
    w = wcol_ref[0]
    for u in range(pps):
        s = _mm_nt(qis, page_refs[u][0, 0])
        o_ref[0, 0, u:u + 1, :] = jnp.sum(jnp.maximum(s, 0.0) * w, axis=0, keepdims=True)


def _paged_scores(page_table, qi, wcol, cache_kidx, layer):
    db, n_pages = page_table.shape
    pps = _pick(n_pages, (32, 16, 8))
    steps = n_pages // pps

    def page_spec(u):
        return pl.BlockSpec((1, 1, PAGE_SIZE, D_IDX), lambda b, j, pt, u=u: (layer, pt[b, j * pps + u], 0, 0))

    return pl.pallas_call(
        functools.partial(_pidx_kernel, pps=pps),
        grid_spec=pltpu.PrefetchScalarGridSpec(
            num_scalar_prefetch=1,
            grid=(db, steps),
            in_specs=[pl.BlockSpec((1, 1, D_B), lambda b, j, pt: (b, 0, 0)),
                      pl.BlockSpec((1, H_IDX, 1), lambda b, j, pt: (b, 0, 0))]
                     + [page_spec(u) for u in range(pps)],
            out_specs=pl.BlockSpec((1, 1, pps, PAGE_SIZE), lambda b, j, pt: (b, j, 0, 0)),
        ),
        out_shape=jax.ShapeDtypeStruct((db, steps, pps, PAGE_SIZE), F32),
        compiler_params=_cparams(("parallel", "arbitrary")),
        name="paged_scores",
    )(page_table, qi, wcol, *([cache_kidx] * pps))


def _psel_kernel(s_ref, qi_ref, kiwi_ref, sel_ref, seln_ref, *, topk):
    db, n_past = s_ref.shape
    past = _sortable(s_ref[...])
    qi = qi_ref[...].astype(F32)
    kiwi = kiwi_ref[...]
    ki = kiwi[:, :D_IDX].astype(BF16).astype(F32)
    s_new = jnp.zeros((db, 1), F32)
    for h in range(H_IDX):
        d = jnp.sum(qi[:, h * D_IDX:(h + 1) * D_IDX] * ki, axis=1, keepdims=True)
        s_new = s_new + jnp.maximum(d, 0.0) * kiwi[:, D_IDX + h:D_IDX + h + 1]
    new = _sortable(s_new)

    def count(pred_past, pred_new):
        return jnp.sum(pred_past.astype(F32), axis=1, keepdims=True) + pred_new.astype(F32)

    count_ge = lambda cand: count(past >= cand, new >= cand)
    thr = _kth_threshold(count_ge, topk, (db, 1))
    n_gt = count(past > thr, new > thr)
    need = topk - n_gt
    kpos = lax.broadcasted_iota(I32, (db, n_past), 1)
    eq = past == thr

    def body(it, c):
        cand = c + lax.shift_left(jnp.int32(1), 15 - it)
        below = jnp.sum((eq & (kpos < cand)).astype(F32), axis=1, keepdims=True)
        return jnp.where(below < need, cand, c)

    cut = lax.fori_loop(0, 16, body, jnp.zeros((db, 1), I32))
    sel_ref[...] = ((past > thr) | (eq & (kpos <= cut))).astype(F32)
    n_eq_past = jnp.sum(eq.astype(F32), axis=1, keepdims=True)
    sel_new = (new > thr) | ((new == thr) & (n_eq_past < need))
    seln_ref[...] = jnp.broadcast_to(sel_new.astype(F32), seln_ref.shape)


def _paged_select(scores, qi, kiwi, topk):
    db, n_past = scores.shape
    return pl.pallas_call(
        functools.partial(_psel_kernel, topk=topk),
        out_shape=[jax.ShapeDtypeStruct((db, n_past), F32), jax.ShapeDtypeStruct((db, LANE), F32)],
        compiler_params=pltpu.CompilerParams(vmem_limit_bytes=VMEM_LIMIT),
        name="paged_select",
    )(scores, qi, kiwi)


def _pattn_kernel(pt_ref, q_ref, sel_ref, seln_ref, kn_ref, vn_ref, exp_ref, *rest, steps, pps):
    del pt_ref
    k_refs = rest[:pps]
    v_refs = rest[pps:2 * pps]
    o_ref, m_scr, l_scr, acc_scr = rest[2 * pps:]
    j = pl.program_id(1)
    q = q_ref[0]
    n_rows = PAGE_SIZE * H_B
    own = (lax.broadcasted_iota(I32, (H_B, n_rows), 1) & (H_B - 1)) == lax.broadcasted_iota(I32, (H_B, n_rows), 0)

    @pl.when(j == 0)
    def _():
        m_scr[...] = jnp.full(m_scr.shape, NEG_BIG, F32)
        l_scr[...] = jnp.zeros(l_scr.shape, F32)
        acc_scr[...] = jnp.zeros(acc_scr.shape, F32)

    s, keep, v2 = [], [], []
    for u in range(pps):
        k2 = k_refs[u][0, 0].reshape(n_rows, HD_B)
        v2.append(v_refs[u][0, 0].reshape(n_rows, HD_B))
        sel = jnp.broadcast_to(sel_ref[0, 0, u:u + 1, :], (H_B, PAGE_SIZE))
        keep_u = (_mm(sel, exp_ref[...]) > 0.5) & own
        keep.append(keep_u)
        s.append(jnp.where(keep_u, _mm_nt(q, k2), NEG_BIG))
    m = m_scr[...]
    m_new = m
    for u in range(pps):
        m_new = jnp.maximum(m_new, jnp.max(s[u], axis=1, keepdims=True))
    alpha = jnp.exp(m - m_new)
    l = alpha * l_scr[...]
    acc = alpha * acc_scr[...]
    for u in range(pps):
        p = jnp.where(keep[u], jnp.exp(s[u] - m_new), 0.0)
        l = l + jnp.sum(p, axis=1, keepdims=True)
        acc = acc + _mm(p, v2[u])
    m_scr[...] = m_new
    l_scr[...] = l
    acc_scr[...] = acc

    @pl.when(j == steps - 1)
    def _():
        s_n = jnp.sum(q * kn_ref[0].astype(BF16).astype(F32), axis=1, keepdims=True)
        keep_n = seln_ref[0][:, :1] > 0.5
        s_n = jnp.where(keep_n, s_n, NEG_BIG)
        m_o = m_scr[...]
        m_f = jnp.maximum(m_o, s_n)
        a_f = jnp.exp(m_o - m_f)
        p_n = jnp.where(keep_n, jnp.exp(s_n - m_f), 0.0)
        l_f = a_f * l_scr[...] + p_n
        o_ref[0] = (a_f * acc_scr[...] + p_n.astype(BF16).astype(F32) * vn_ref[0].astype(BF16).astype(F32)) / l_f


def _paged_attend(page_table, q, sel, sel_new, k_new, v_new, expand, cache_k, cache_v, layer):
    db, n_pages = page_table.shape
    pps = _pick(n_pages, (8,))
    steps = n_pages // pps

    def page_spec(u):
        return pl.BlockSpec((1, 1, PAGE_SIZE, H_B, HD_B),
                            lambda b, j, pt, u=u: (layer, pt[b, j * pps + u], 0, 0, 0))

    vec = lambda shape: pl.BlockSpec((1,) + shape, lambda b, j, pt: (b,) + (0,) * len(shape))
    return pl.pallas_call(
        functools.partial(_pattn_kernel, steps=steps, pps=pps),
        grid_spec=pltpu.PrefetchScalarGridSpec(
            num_scalar_prefetch=1,
            grid=(db, steps),
            in_specs=[vec((H_B, HD_B)),
                      pl.BlockSpec((1, 1, pps, PAGE_SIZE), lambda b, j, pt: (b, j, 0, 0)),
                      vec((1, LANE)), vec((H_B, HD_B)), vec((H_B, HD_B)),
                      pl.BlockSpec(expand.shape, lambda b, j, pt: (0, 0))]
                     + [page_spec(u) for u in range(pps)] * 2,
            out_specs=vec((H_B, HD_B)),
            scratch_shapes=[pltpu.VMEM((H_B, 1), F32), pltpu.VMEM((H_B, 1), F32), pltpu.VMEM((H_B, HD_B), F32)],
        ),
        out_shape=jax.ShapeDtypeStruct((db, H_B, HD_B), F32),
        compiler_params=_cparams(("parallel", "arbitrary")),
        name="paged_attend",
    )(page_table, q, sel, sel_new, k_new, v_new, expand, *([cache_k] * pps), *([cache_v] * pps))


def _rwkv_pre(pc, prev, mu, w0, w2, a0, a2, g2, ckk, cka, seg):
    xm = pc + (prev - pc) * mu
    r = xm[:, 0:D_C]
    k = xm[:, D_C:2 * D_C]
    v = xm[:, 2 * D_C:3 * D_C]
    o = 3 * D_C
    lw = xm[:, o:o + LORA_W]
    la = xm[:, o + LORA_W:o + LORA_W + LORA_A]
    lg = xm[:, o + LORA_W + LORA_A:]
    z = -(w0 + _mm(jnp.tanh(lw), w2))
    softplus = jnp.maximum(z, 0.0) + jnp.log(1.0 + jnp.exp(-jnp.abs(z)))
    log_decay = -jnp.exp(-softplus - 0.5)
    a = _sigmoid(a0 + _mm(la, a2))
    g = _mm(_sigmoid(lg), g2)
    kk = k * ckk
    kk = kk * lax.rsqrt(jnp.maximum(_mm_exact_lhs(kk * kk, seg), 1e-24))
    kmod = k * (1.0 + (a - 1.0) * cka)
    return r, log_decay, kmod, v, kk, a, g


def _rwkv_post(y, r, kmod, v, g, rk, lnx_g, lnx_b, seg):
    mu = _mm_exact_lhs(y, seg) * (1.0 / N_C)
    yc = y - mu
    var = _mm_exact_lhs(yc * yc, seg) * (1.0 / N_C)
    y = yc * lax.rsqrt(var + LNX_EPS) * lnx_g + lnx_b
    y = y + _mm_exact_lhs(r * kmod * rk, seg) * v
    return y * g


def _rwkv_kernel(pc_ref, mu_ref, w0_ref, w2_ref, a0_ref, a2_ref, g2_ref, ckk_ref, cka_ref,
                 rk_ref, lng_ref, lnb_ref, seg_ref, tril_ref,
                 y_ref, s_ref, h_scr, last_scr, *, n_chunks):
    c_len = SCAN_CHUNK
    ti = pl.program_id(1)

    @pl.when(ti == 0)
    def _():
        h_scr[...] = jnp.zeros(h_scr.shape, F32)
        last_scr[...] = jnp.zeros(last_scr.shape, F32)

    seg = seg_ref[...]
    tril = tril_ref[...]
    rr = lax.broadcasted_iota(I32, (c_len, c_len), 0)
    cc = lax.broadcasted_iota(I32, (c_len, c_len), 1)
    strict = rr > cc
    incl = rr >= cc
    eye = (rr == cc).astype(F32)
    heads = range(H_C)
    hsl = [slice(h * N_C, (h + 1) * N_C) for h in heads]

    def chunk(c, carry):
        rows = pl.ds(pl.multiple_of(c * c_len, c_len), c_len)
        pc = pc_ref[0, rows, :]
        first = lax.broadcasted_iota(I32, pc.shape, 0) == 0
        prev = jnp.where(first, jnp.broadcast_to(last_scr[...], pc.shape), pltpu.roll(pc, 1, 0))
        last_scr[...] = pc[c_len - 1:c_len, :]
        r, logw, kmod, v, kk, a, g = _rwkv_pre(
            pc, prev, mu_ref[...], w0_ref[...], w2_ref[...], a0_ref[...],
            a2_ref[...], g2_ref[...], ckk_ref[...], cka_ref[...], seg)
        cum = _mm_exact_rhs(tril, logw)
        p_incl = jnp.exp(cum)
        p_excl = jnp.exp(cum - logw)
        inv = jnp.exp(-cum)
        at = -kk * p_excl
        rt = r * p_incl
        p_end = p_incl[c_len - 1:c_len, :]
        bt = kk * a * inv
        kt = kmod * inv
        bte = bt * p_end
        kte = kt * p_end
        ar = [jnp.concatenate([at[:, hs], rt[:, hs]], axis=0) for hs in hsl]
        gb = [_mm_nt(ar[h], bt[:, hsl[h]]) for h in heads]
        gk = [_mm_nt(ar[h], kt[:, hsl[h]]) for h in heads]
        lmat = [jnp.where(strict, gb[h][:c_len], 0.0) for h in heads]
        akm = [jnp.where(strict, gk[h][:c_len], 0.0) for h in heads]
        rbm = [jnp.where(incl, gb[h][c_len:], 0.0) for h in heads]
        rkm = [jnp.where(incl, gk[h][c_len:], 0.0) for h in heads]
        tinv = [eye + lmat[h] for h in heads]
        pw = lmat
        for _ in range(5):
            pw = [_mm(pw[h], pw[h]) for h in heads]
            tinv = [tinv[h] + _mm(tinv[h], pw[h]) for h in heads]
        s0 = [h_scr[h] for h in heads]
        vh = [v[:, hs] for hs in hsl]
        x0 = [_mm_nt(at[:, hsl[h]], s0[h]) + _mm(akm[h], vh[h]) for h in heads]
        u = [_mm(tinv[h], x0[h]) for h in heads]
        ys = [_mm_nt(rt[:, hsl[h]], s0[h]) + _mm(rbm[h], u[h]) + _mm(rkm[h], vh[h]) for h in heads]
        for h in heads:
            h_scr[h] = s0[h] * p_end[:, hsl[h]] + _mm_tn(u[h], bte[:, hsl[h]]) + _mm_tn(vh[h], kte[:, hsl[h]])
        y = jnp.concatenate(ys, axis=1)
        y_ref[0, rows, :] = _rwkv_post(y, r, kmod, v, g, rk_ref[...], lng_ref[...], lnb_ref[...], seg)
        return carry

    lax.fori_loop(0, n_chunks, chunk, 0)

    @pl.when(ti == pl.num_programs(1) - 1)
    def _():
        s_ref[0] = h_scr[...]


def _rwkv(p, cp, tm):
    b, t, _ = p.shape
    n_chunks = tm // SCAN_CHUNK
    const = lambda a: pl.BlockSpec(a.shape, lambda bi, i: (0,) * a.ndim)
    consts = [cp[n] for n in ("mu", "w0", "w2", "a0", "a2", "g2", "ckk", "cka", "rk", "lng", "lnb", "seg", "tril")]
    return pl.pallas_call(
        functools.partial(_rwkv_kernel, n_chunks=n_chunks),
        grid=(b, t // tm),
        in_specs=[pl.BlockSpec((1, tm, C_IN), lambda bi, i: (bi, i, OFF_C // C_IN))] + [const(a) for a in consts],
        out_specs=[pl.BlockSpec((1, tm, D_C), lambda bi, i: (bi, i, 0)),
                   pl.BlockSpec((1, H_C, N_C, N_C), lambda bi, i: (bi, 0, 0, 0))],
        out_shape=[jax.ShapeDtypeStruct((b, t, D_C), F32), jax.ShapeDtypeStruct((b, H_C, N_C, N_C), F32)],
        scratch_shapes=[pltpu.VMEM((H_C, N_C, N_C), F32), pltpu.VMEM((1, C_IN), F32)],
        compiler_params=_cparams(("parallel", "arbitrary")),
        name="rwkv",
    )(p, *consts)


def _rwkv_pre_kernel(pc_ref, prev_ref, mu_ref, w0_ref, w2_ref, a0_ref, a2_ref, g2_ref, ckk_ref, cka_ref,
                     rk_ref, seg_ref, r_ref, w_ref, k_ref, v_ref, a_ref, b_ref, g_ref, rkv_ref):
    seg = seg_ref[...]
    r, logw, kmod, v, kk, a, g = _rwkv_pre(
        pc_ref[...], prev_ref[...], mu_ref[...], w0_ref[...], w2_ref[...], a0_ref[...], a2_ref[...],
        g2_ref[...], ckk_ref[...], cka_ref[...], seg)
    r_ref[...] = r
    w_ref[...] = jnp.exp(logw)
    k_ref[...] = kmod
    v_ref[...] = v
    a_ref[...] = -kk
    b_ref[...] = kk * a
    g_ref[...] = g
    rkv_ref[...] = _mm_exact_lhs(r * kmod * rk_ref[...], seg) * v


def _rwkv_pre_call(p, prev, cp):
    m = prev.shape[0]
    consts = [cp[n] for n in ("mu", "w0", "w2", "a0", "a2", "g2", "ckk", "cka", "rk", "seg")]
    const = lambda a: pl.BlockSpec(a.shape, lambda i: (0,) * a.ndim)
    return pl.pallas_call(
        _rwkv_pre_kernel,
        grid=(1,),
        in_specs=[pl.BlockSpec((m, C_IN), lambda i: (0, OFF_C // C_IN)),
                  pl.BlockSpec((m, C_IN), lambda i: (0, 0))] + [const(a) for a in consts],
        out_specs=[pl.BlockSpec((m, D_C), lambda i: (0, 0))] * 8,
        out_shape=[jax.ShapeDtypeStruct((m, D_C), F32)] * 8,
        name="rwkv_pre",
    )(p, prev, *consts)


def _rwkv_step_kernel(s_ref, r_ref, w_ref, k_ref, a_ref, b_ref, v_ref, g_ref, rkv_ref, lng_ref, lnb_ref,
                      y_ref, so_ref):
    s = s_ref[0]
    sa = jnp.sum(s * a_ref[0], axis=-1, keepdims=True)
    v = v_ref[0]
    s = s * w_ref[0] + sa * b_ref[0] + v * k_ref[0]
    so_ref[0] = s
    y = jnp.sum(s * r_ref[0], axis=-1, keepdims=True)
    mu = jnp.mean(y, axis=1, keepdims=True)
    yc = y - mu
    var = jnp.mean(yc * yc, axis=1, keepdims=True)
    y = yc * lax.rsqrt(var + LNX_EPS) * lng_ref[...] + lnb_ref[...]
    y_ref[0] = (y + rkv_ref[0]) * g_ref[0]


def _rwkv_step(state, rows, cols, lng_col, lnb_col):
    db = state.shape[0]
    rspec = pl.BlockSpec((1, H_C, 1, N_C), lambda b: (b, 0, 0, 0))
    cspec = pl.BlockSpec((1, H_C, N_C, 1), lambda b: (b, 0, 0, 0))
    pspec = pl.BlockSpec((H_C, N_C, 1), lambda b: (0, 0, 0))
    sspec = pl.BlockSpec((1, H_C, N_C, N_C), lambda b: (b, 0, 0, 0))
    return pl.pallas_call(
        _rwkv_step_kernel,
        grid=(db,),
        in_specs=[sspec] + [rspec] * 5 + [cspec] * 3 + [pspec] * 2,
        out_specs=[cspec, sspec],
        out_shape=[jax.ShapeDtypeStruct((db, H_C, N_C, 1), F32), jax.ShapeDtypeStruct(state.shape, F32)],
        compiler_params=_cparams(("parallel",)),
        name="rwkv_step",
    )(state, *rows, *cols, lng_col, lnb_col)


def _merge_kernel(x_ref, ya_ref, yb_ref, yc_ref, ga_ref, gb_ref, gc_ref, wa_ref, wb_ref, wc_ref, wo_ref, g_ref,
                  o_ref):
    merged = (_sigmoid(ga_ref[...]) * _mm(ya_ref[...], wa_ref[...])
              + _sigmoid(gb_ref[...]) * _mm(yb_ref[...], wb_ref[...])
              + _sigmoid(gc_ref[...]) * _mm(yc_ref[...], wc_ref[...]))
    o_ref[...] = x_ref[...] + _rms(_mm(merged, wo_ref[...]), g_ref[...])


def _merge(x, ya, yb, yc, p, wa, wb, wc, wo, g, tm):
    m = x.shape[0]
    row = lambda w: pl.BlockSpec((tm, w), lambda i: (i, 0))
    gate = lambda n: pl.BlockSpec((tm, D_MODEL), lambda i, n=n: (i, OFF_GATE // D_MODEL + n))
    const = lambda a: pl.BlockSpec(a.shape, lambda i: (0,) * a.ndim)
    return pl.pallas_call(
        _merge_kernel,
        grid=(m // tm,),
        in_specs=[row(D_MODEL), row(D_A), row(D_B), row(D_C), gate(0), gate(1), gate(2),
                  const(wa), const(wb), const(wc), const(wo), const(g)],
        out_specs=row(D_MODEL),
        out_shape=jax.ShapeDtypeStruct((m, D_MODEL), F32),
        compiler_params=_cparams(("parallel",)),
        name="merge",
    )(x, ya, yb, yc, p, p, p, wa, wb, wc, wo, g)


def _ffn_kernel(x_ref, gpre_ref, w1_ref, w3_ref, w2_ref, gpost_ref, o_ref, h_scr, acc_scr):
    j = pl.program_id(1)

    @pl.when(j == 0)
    def _():
        h_scr[...] = _rms(x_ref[...], gpre_ref[...]).astype(BF16)
        acc_scr[...] = jnp.zeros(acc_scr.shape, F32)

    h = h_scr[...]
    f1 = jnp.dot(h, w1_ref[...], preferred_element_type=F32)
    f3 = jnp.dot(h, w3_ref[...], preferred_element_type=F32)
    acc_scr[...] += _mm(f1 * _sigmoid(f1) * f3, w2_ref[...])

    @pl.when(j == pl.num_programs(1) - 1)
    def _():
        o_ref[...] = x_ref[...] + _rms(acc_scr[...], gpost_ref[...])


def _ffn(x, gpre, w_in, w_out, gpost, tm, nf):
    m = x.shape[0]
    tf = D_FF // nf
    return pl.pallas_call(
        _ffn_kernel,
        grid=(m // tm, nf),
        in_specs=[pl.BlockSpec((tm, D_MODEL), lambda i, j: (i, 0)),
                  pl.BlockSpec((1, D_MODEL), lambda i, j: (0, 0)),
                  pl.BlockSpec((D_MODEL, tf), lambda i, j: (0, j)),
                  pl.BlockSpec((D_MODEL, tf), lambda i, j: (0, nf + j)),
                  pl.BlockSpec((tf, D_MODEL), lambda i, j: (j, 0)),
                  pl.BlockSpec((1, D_MODEL), lambda i, j: (0, 0))],
        out_specs=pl.BlockSpec((tm, D_MODEL), lambda i, j: (i, 0)),
        out_shape=jax.ShapeDtypeStruct((m, D_MODEL), F32),
        scratch_shapes=[pltpu.VMEM((tm, D_MODEL), BF16), pltpu.VMEM((tm, D_MODEL), F32)],
        compiler_params=_cparams(("parallel", "arbitrary")),
        name="ffn",
    )(x, gpre, w_in, w_in, w_out, gpost)


def _pad_w_in(w):
    s = {}
    o = 0
    for name, width in (("au", D_A), ("av", D_A), ("q", D_B), ("k", D_B), ("v", D_B), ("qi", H_IDX * D_IDX),
                        ("ki", D_IDX), ("wi", H_IDX), ("c", C_IN), ("gate", 3 * D_MODEL)):
        s[name] = w[:, o:o + width]
        o += width
    z = lambda n: jnp.zeros((w.shape[0], n), w.dtype)
    cols = [s["c"], z(OFF_GATE - C_IN), s["gate"], s["au"], s["av"], s["q"], s["k"], s["v"], s["qi"],
            s["ki"], s["wi"], z(LANE - D_IDX - H_IDX)]
    return jnp.concatenate(cols, axis=1).astype(BF16)


def _rope_tables(pos):
    half = HD_B // 2
    inv_freq = ROPE_THETA ** (-jnp.arange(half, dtype=F32) / half)
    ang = pos.astype(F32)[:, None] * inv_freq[None, :]
    cos = jnp.cos(ang)
    sin = jnp.sin(ang)
    return jnp.concatenate([cos] * 4, axis=1), jnp.concatenate([-sin, sin] * 2, axis=1)


def _layer_params(l, P):
    row = lambda a: a[l].reshape(1, -1)
    eye_h = jnp.repeat(jnp.repeat(jnp.eye(H_C, dtype=F32), N_C, axis=0), N_C, axis=1)
    causal = jnp.tril(jnp.ones((CHUNK, CHUNK), dtype=bool))
    ws_tril = jnp.where(causal[None], P["a_ws"][l], 0.0)
    cp = dict(mu=row(P["c_mu"]), w0=row(P["c_w0"]), w2=P["c_w2"][l].astype(BF16), a0=row(P["c_a0"]),
              a2=P["c_a2"][l].astype(BF16), g2=P["c_g2"][l].astype(BF16), ckk=row(P["c_kk"]), cka=row(P["c_ka"]),
              rk=row(P["c_rk"]), lng=row(P["c_lnx_g"]), lnb=row(P["c_lnx_b"]), seg=eye_h.astype(BF16),
              tril=jnp.tril(jnp.ones((SCAN_CHUNK, SCAN_CHUNK), F32)).astype(BF16))
    return dict(
        g_pre_mix=row(P["g_pre_mix"]), g_post_mix=row(P["g_post_mix"]),
        g_pre_ffn=row(P["g_pre_ffn"]), g_post_ffn=row(P["g_post_ffn"]),
        w_in=_pad_w_in(P["w_in"][l]),
        a_ln_g=row(P["a_ln_g"]), a_ln_b=row(P["a_ln_b"]),
        ws_tril=ws_tril.astype(BF16),
        bs_full=jnp.repeat(P["a_bs"][l].T, D_A // G_A, axis=1),
        ws00=jnp.repeat(P["a_ws"][l][:, 0, 0], D_A // G_A).reshape(1, -1),
        bs0=jnp.repeat(P["a_bs"][l][:, 0], D_A // G_A).reshape(1, -1),
        cp=cp,
        lng_col=P["c_lnx_g"][l].reshape(H_C, N_C, 1), lnb_col=P["c_lnx_b"][l].reshape(H_C, N_C, 1),
        w_br_a=P["w_br_a"][l].astype(BF16), w_br_b=P["w_br_b"][l].astype(BF16), w_br_c=P["w_br_c"][l].astype(BF16),
        w_out=P["w_out"][l].astype(BF16), w_ffn_in=P["w_ffn_in"][l].astype(BF16),
        w_ffn_out=P["w_ffn_out"][l].astype(BF16))


def _pick(n, prefs):
    for c in prefs:
        if n % c == 0:
            return c
    return n


def _finish_layer(x2d, ya, yb, yc, p2d, L, tm):
    x1 = _merge(x2d, ya, yb, yc, p2d, L["w_br_a"], L["w_br_b"], L["w_br_c"], L["w_out"], L["g_post_mix"], tm)
    return _ffn(x1, L["g_pre_ffn"], L["w_ffn_in"], L["w_ffn_out"], L["g_post_ffn"], tm, 2)


def _prompt_layer(x, L, cos, sin):
    b, t, _ = x.shape
    m = b * t
    x2d = x.reshape(m, D_MODEL)
    p2d = _proj(x2d, L["g_pre_mix"], L["w_in"], _pick(m, (1024, 512, 256, 128)), N_PAD // 5)
    p = p2d.reshape(b, t, N_PAD)
    ya = _sgu(p, L["a_ln_g"], L["a_ln_b"], L["ws_tril"], L["bs_full"], _pick(t // CHUNK, (4, 2, 1)))
    qt, k, kb, vt, qit, kiwi, kiwit = _rope_tiles(p, cos, sin, _pick(t, (512, 256, 128)))
    yb = _dsa(qt, qit, kiwit, kiwi, kb, vt)
    yc, wkv = _rwkv(p, L["cp"], _pick(t, (512, 256, 128, 64)))
    y = _finish_layer(x2d, ya.reshape(m, D_A), yb.reshape(m, D_B), yc.reshape(m, D_C), p2d, L,
                      _pick(m, (512, 256, 128)))
    v = p[:, :, OFF_V:OFF_V + D_B]
    return (y.reshape(b, t, D_MODEL), k.reshape(b, t, H_B, HD_B), v.reshape(b, t, H_B, HD_B),
            kiwi[:, :, :D_IDX], p[:, -1, OFF_C:OFF_C + C_IN], wkv)


def _sample_layer(x, L, cos, sin, layer, cache_k, cache_v, cache_kidx, page_table, c_prev, wkv0, expand):
    db, t, _ = x.shape
    x2d = x.reshape(db, D_MODEL)
    p2d = _proj(x2d, L["g_pre_mix"], L["w_in"], db, N_PAD // 5)
    ya, va = _sgu_step(p2d, L["a_ln_g"], L["a_ln_b"], L["ws00"], L["bs0"])
    q, k, qi, kiwi = _rope_rows(p2d[None], cos, sin, db)
    v = p2d[:, OFF_V:OFF_V + D_B]
    n_pages = page_table.shape[1]
    n_past = n_pages * PAGE_SIZE
    topk = min(TOPK_MAX, (n_past + t) // 4)
    wcol = kiwi[0, :, D_IDX:D_IDX + H_IDX].reshape(db, H_IDX, 1)
    scores = _paged_scores(page_table, qi.reshape(db, 1, D_B), wcol, cache_kidx, layer)
    sel, sel_new = _paged_select(scores.reshape(db, n_past), qi[0], kiwi[0], topk)
    pps = _pick(n_pages, (8,))
    yb = _paged_attend(page_table, q.reshape(db, H_B, HD_B), sel.reshape(db, n_pages // pps, pps, PAGE_SIZE),
                       sel_new.reshape(db, 1, LANE), k.reshape(db, H_B, HD_B), v.reshape(db, H_B, HD_B),
                       expand, cache_k, cache_v, layer)
    r, w, kmod, v_c, a_c, b_c, g, rkv = _rwkv_pre_call(p2d, c_prev, L["cp"])
    rowv = lambda a: a.reshape(db, H_C, 1, N_C)
    colv = lambda a: a.reshape(db, H_C, N_C, 1)
    yc, wkv = _rwkv_step(wkv0, [rowv(r), rowv(w), rowv(kmod), rowv(a_c), rowv(b_c)],
                         [colv(v_c), colv(g), colv(rkv)], L["lng_col"], L["lnb_col"])
    y = _finish_layer(x2d, ya, yb.reshape(db, D_B), yc.reshape(db, D_C), p2d, L, db)
    return (y.reshape(db, t, D_MODEL), k.reshape(db, t, H_B, HD_B), v.reshape(db, t, H_B, HD_B),
            kiwi[0, :, :D_IDX].reshape(db, t, D_IDX), va.reshape(db, t, D_A),
            p2d[:, OFF_C:OFF_C + C_IN], wkv)


def kernel(x_prompt, x_sample, cache_k, cache_v, cache_kidx, state_c_shift, state_c_wkv, page_table,
           g_pre_mix, g_post_mix, g_pre_ffn, g_post_ffn, w_in, a_ln_g, a_ln_b, a_ws, a_bs,
           c_mu, c_w0, c_w2, c_a0, c_a2, c_g2, c_kk, c_ka, c_rk, c_lnx_g, c_lnx_b,
           w_br_a, w_br_b, w_br_c, w_out, w_ffn_in, w_ffn_out):
    P = dict(g_pre_mix=g_pre_mix, g_post_mix=g_post_mix, g_pre_ffn=g_pre_ffn, g_post_ffn=g_post_ffn, w_in=w_in,
             a_ln_g=a_ln_g, a_ln_b=a_ln_b, a_ws=a_ws, a_bs=a_bs, c_mu=c_mu, c_w0=c_w0, c_w2=c_w2, c_a0=c_a0,
             c_a2=c_a2, c_g2=c_g2, c_kk=c_kk, c_ka=c_ka, c_rk=c_rk, c_lnx_g=c_lnx_g, c_lnx_b=c_lnx_b,
             w_br_a=w_br_a, w_br_b=w_br_b, w_br_c=w_br_c, w_out=w_out, w_ffn_in=w_ffn_in, w_ffn_out=w_ffn_out)
    depth = w_in.shape[0]
    t = x_prompt.shape[1]
    db = x_sample.shape[0]
    n_past = page_table.shape[1] * PAGE_SIZE
    cos_p, sin_p = _rope_tables(jnp.arange(t, dtype=jnp.int32))
    cos_s, sin_s = _rope_tables(jnp.full((db,), n_past, dtype=jnp.int32))
    expand = jnp.repeat(jnp.eye(PAGE_SIZE, dtype=F32), H_B, axis=1).astype(BF16)
    hp, hs = x_prompt, x_sample
    outs_p, outs_s = [], []
    for l in range(depth):
        L = _layer_params(l, P)
        hp, *rest_p = _prompt_layer(hp, L, cos_p, sin_p)
        outs_p.append(rest_p)
        hs, *rest_s = _sample_layer(hs, L, cos_s, sin_s, l, cache_k, cache_v, cache_kidx, page_table,
                                    state_c_shift[l], state_c_wkv[l], expand)
        outs_s.append(rest_s)
    stack = lambda outs, i: jnp.stack([o[i] for o in outs])
    return (hp, hs,
            stack(outs_p, 0), stack(outs_p, 1), stack(outs_p, 2), stack(outs_p, 3), stack(outs_p, 4),
            stack(outs_s, 0), stack(outs_s, 1), stack(outs_s, 2), stack(outs_s, 3), stack(outs_s, 4),
            stack(outs_s, 5))
```
